```python
import jax, jax.numpy as jnp
from jax import lax
import numpy as np

D_MODEL = 1024
BATCH = 1
SEQ = 16384
DEPTH = 2
DEC_BATCH = 32
DEC_SEQ = 8
PAST_LEN = 16384
PAGE_SIZE = 128

HEAD_DIM = 64
ROT_DIM = HEAD_DIM // 4
ROPE_THETA = 500000.0
NORM_EPS = 1e-6
Q_BLOCK = 128
NEG_INF = -1e30

NSA_HEADS = D_MODEL // HEAD_DIM
NSA_KV_HEADS = 4
NSA_GROUP = NSA_HEADS // NSA_KV_HEADS
NSA_INNER = NSA_HEADS * HEAD_DIM
NSA_KV_WIDTH = NSA_KV_HEADS * HEAD_DIM
CMP_BLOCK = 32
CMP_STRIDE = 16
CMP_HIDDEN = HEAD_DIM
SEL_BLOCK = 64
SEL_TOP_N = 16
NSA_WINDOW = 512
FORCE_BONUS = 1e4
N_PAGED_STREAMS = 4
NSA_IN_WIDTH = NSA_INNER + 6 * NSA_KV_WIDTH + 3 * NSA_HEADS + NSA_INNER

DIL_PAIRS = ((128, 1), (512, 4), (2048, 16))
N_DIL_GROUPS = len(DIL_PAIRS)
DIL_HEADS = D_MODEL // HEAD_DIM
DIL_INNER = DIL_HEADS * HEAD_DIM
DIL_IN_WIDTH = 3 * N_DIL_GROUPS * DIL_INNER + DIL_INNER

N_NSA_LAYERS = (DEPTH + 1) // 2
N_DIL_LAYERS = DEPTH // 2

kernel_name = 'nsa_dilated_hybrid_decode_step'


def _split(a, sizes):
    out, start = [], 0
    for n in sizes:
        out.append(a[..., start:start + n])
        start += n
    return out


def rms_norm(x, g):
    x32 = x.astype(jnp.float32)
    y = x32 * lax.rsqrt(jnp.mean(x32 * x32, axis=-1, keepdims=True) + NORM_EPS)
    return (y * g.astype(jnp.float32)).astype(x.dtype)


def partial_rope(x, pos):
    half = ROT_DIM // 2
    inv = ROPE_THETA ** (-jnp.arange(half, dtype=jnp.float32) / half)
    ang = pos.astype(jnp.float32)[:, None] * inv[None, :]
    cos = jnp.cos(ang)[None, :, None, :]
    sin = jnp.sin(ang)[None, :, None, :]
    xr = x[..., :ROT_DIM].astype(jnp.float32)
    x1, x2 = xr[..., :half], xr[..., half:]
    rot = jnp.concatenate([x1 * cos - x2 * sin, x2 * cos + x1 * sin], axis=-1)
    return jnp.concatenate([rot.astype(x.dtype), x[..., ROT_DIM:]], axis=-1)


def masked_softmax(s, mask, axis):
    s = jnp.where(mask, s, NEG_INF)
    m = jnp.max(s, axis=axis, keepdims=True)
    e = jnp.where(mask, jnp.exp(s - m), 0.0)
    den = jnp.sum(e, axis=axis, keepdims=True)
    return e / jnp.maximum(den, 1e-30), m, den


def nsa_project(h, w_in, pos):
    B, T, _ = h.shape
    q, kc, vc, ks, vs, kw, vw, gl, z = _split(
        h @ w_in, [NSA_INNER] + [NSA_KV_WIDTH] * 6 + [3 * NSA_HEADS, NSA_INNER])
    kv = lambda a: a.reshape(B, T, NSA_KV_HEADS, HEAD_DIM)
    grp = lambda a: a.reshape(B, T, NSA_KV_HEADS, NSA_GROUP, HEAD_DIM)
    q = q.reshape(B, T, NSA_HEADS, HEAD_DIM)
    q_rot = partial_rope(q, pos)
    gate = jax.nn.sigmoid(gl.astype(jnp.float32)).reshape(
        B, T, NSA_KV_HEADS, NSA_GROUP, 3).astype(h.dtype)
    rows = jnp.stack([kv(kc), kv(vc), partial_rope(kv(ks), pos), kv(vs)], axis=2)
    win = jnp.stack([partial_rope(kv(kw), pos), kv(vw)], axis=2)
    return grp(q), grp(q_rot), gate, z, rows, win


def compress_rows(rows, w1, b1, w2, pos_emb):
    B, L, KVH, HD = rows.shape
    R = CMP_BLOCK // CMP_STRIDE
    n_chunk = L // CMP_STRIDE
    n_cmp = n_chunk - R + 1
    chunks = rows[:, :n_chunk * CMP_STRIDE].reshape(B, n_chunk, CMP_STRIDE, KVH, HD)
    w1r = w1.reshape(R, CMP_STRIDE, HD, CMP_HIDDEN)
    part = jnp.einsum('bnsgd,rsde->rbnge', chunks, w1r)
    pre = b1 + jnp.einsum('rsd,rsde->e', pos_emb.reshape(R, CMP_STRIDE, HD), w1r)
    for r in range(R):
        pre = pre + part[r, :, r:r + n_cmp]
    summ = jax.nn.silu(pre) @ w2
    c_end = jnp.arange(n_cmp, dtype=jnp.int32) * CMP_STRIDE + (CMP_BLOCK - 1)
    return summ, c_end


def sel_blocks(rows):
    B, L, KVH, HD = rows.shape
    n_sel = -(-L // SEL_BLOCK)
    pad = n_sel * SEL_BLOCK - L
    if pad:
        rows = jnp.pad(rows, ((0, 0), (0, pad), (0, 0), (0, 0)))
    return rows.reshape(B, n_sel, SEL_BLOCK, KVH, HD)


def nsa_attend(q_raw, q_rot, gate, kc, vc, c_end, kb, vb, kw, vw, w_pos, q_pos):
    f32 = jnp.float32
    scale = HEAD_DIM ** -0.5
    B, Tq, KVH, G, HD = q_raw.shape
    s_c = jnp.einsum('btgrd,bcgd->btgrc', q_raw, kc).astype(f32) * scale
    m_c = (c_end[None, :] <= q_pos[:, None])[None, :, None, None, :]
    p_c, _, _ = masked_softmax(s_c, m_c, -1)
    o_c = jnp.einsum('btgrc,bcgd->btgrd', p_c.astype(vc.dtype), vc)
    n_cmp = kc.shape[1]
    n_sel = kb.shape[1]
    imp = jnp.sum(p_c, axis=3)
    lo = -((CMP_BLOCK - 1) // CMP_STRIDE)
    hi = (SEL_BLOCK - 1) // CMP_STRIDE
    ci = (SEL_BLOCK // CMP_STRIDE) * jnp.arange(n_sel)[:, None] + jnp.arange(lo, hi + 1)[None, :]
    ok = (ci >= 0) & (ci < n_cmp)
    imp_s = jnp.sum(jnp.where(ok, jnp.take(imp, jnp.clip(ci, 0, n_cmp - 1), axis=-1), 0.0), axis=-1)
    blk = jnp.arange(n_sel, dtype=jnp.int32)[None, :]
    cur = (q_pos // SEL_BLOCK)[:, None]
    valid = blk * SEL_BLOCK <= q_pos[:, None]
    forced = (blk == 0) | (blk == cur) | (blk == cur - 1)
    score = (jnp.where(valid[None, :, None, :], imp_s, -1.0)
             + jnp.where(forced, FORCE_BONUS, 0.0)[None, :, None, :])
    n_top = min(SEL_TOP_N, n_sel)
    _, sel = lax.top_k(score, n_top)
    bi = jnp.arange(B)[:, None, None]
    gi = jnp.arange(KVH)[None, :, None]
    sel_t = jnp.transpose(sel, (0, 2, 1, 3)).reshape(B, KVH, Tq * n_top)
    ks = kb[bi, sel_t, :, gi].reshape(B, KVH, Tq, n_top, SEL_BLOCK, HD)
    vs = vb[bi, sel_t, :, gi].reshape(B, KVH, Tq, n_top, SEL_BLOCK, HD)
    s_s = jnp.einsum('btgrd,bgtnsd->btgrns', q_rot, ks).astype(f32) * scale
    k_pos = sel[..., None] * SEL_BLOCK + jnp.arange(SEL_BLOCK, dtype=jnp.int32)
    m_s = (k_pos <= q_pos[None, :, None, None, None])[:, :, :, None]
    p_s, _, _ = masked_softmax(s_s, m_s, (-2, -1))
    o_s = jnp.einsum('btgrns,bgtnsd->btgrd', p_s.astype(vs.dtype), vs)
    s_w = jnp.einsum('btgrd,bkgd->btgrk', q_rot, kw).astype(f32) * scale
    dpos = q_pos[:, None] - w_pos[None, :]
    m_w = ((dpos >= 0) & (dpos < NSA_WINDOW) & (w_pos[None, :] >= 0))[None, :, None, None, :]
    p_w, _, _ = masked_softmax(s_w, m_w, -1)
    o_w = jnp.einsum('btgrk,bkgd->btgrd', p_w.astype(vw.dtype), vw)
    o = gate[..., 0:1] * o_c + gate[..., 1:2] * o_s + gate[..., 2:3] * o_w
    return o.astype(q_raw.dtype)


def nsa_prompt_layer(x, g, w_in, cmp_w1, cmp_b1, cmp_w2, cmp_pos, w_out):
    B, T, _ = x.shape
    pos = jnp.arange(T, dtype=jnp.int32)
    q, q_rot, gate, z, rows, win = nsa_project(rms_norm(x, g), w_in, pos)
    kc, c_end = compress_rows(rows[:, :, 0], cmp_w1[0], cmp_b1[0], cmp_w2[0], cmp_pos[0])
    vc, _ = compress_rows(rows[:, :, 1], cmp_w1[1], cmp_b1[1], cmp_w2[1], cmp_pos[1])
    kb = sel_blocks(rows[:, :, 2])
    vb = sel_blocks(rows[:, :, 3])
    winp = jnp.pad(win, ((0, 0), (NSA_WINDOW, 0), (0, 0), (0, 0), (0, 0)))

    def q_block(i):
        s = i * Q_BLOCK
        take = lambda a: lax.dynamic_slice_in_dim(a, s, Q_BLOCK, axis=1)
        seg = lax.dynamic_slice_in_dim(winp, s, NSA_WINDOW + Q_BLOCK, axis=1)
        q_pos = s + jnp.arange(Q_BLOCK, dtype=jnp.int32)
        w_pos = s - NSA_WINDOW + jnp.arange(NSA_WINDOW + Q_BLOCK, dtype=jnp.int32)
        return nsa_attend(take(q), take(q_rot), take(gate), kc, vc, c_end, kb, vb,
                          seg[:, :, 0], seg[:, :, 1], w_pos, q_pos)

    o = lax.map(q_block, jnp.arange(T // Q_BLOCK, dtype=jnp.int32))
    o = jnp.moveaxis(o, 0, 1).reshape(B, T, NSA_INNER)
    y = x + (o * jax.nn.silu(z)) @ w_out
    return y, rows, win[:, T - min(NSA_WINDOW, T):]


def nsa_sample_layer(x, cache, li, page_table, win_buf, g, w_in, cmp_w1, cmp_b1, cmp_w2, cmp_pos, w_out):
    B, T, _ = x.shape
    n_past = page_table.shape[1] * PAGE_SIZE
    pos = n_past + jnp.arange(T, dtype=jnp.int32)
    q, q_rot, gate, z, rows, win = nsa_project(rms_norm(x, g), w_in, pos)
    past = cache[li, page_table].reshape(B, n_past, N_PAGED_STREAMS, NSA_KV_HEADS, HEAD_DIM)
    full = jnp.concatenate([past, rows], axis=1)
    kc, c_end = compress_rows(full[:, :, 0], cmp_w1[0], cmp_b1[0], cmp_w2[0], cmp_pos[0])
    vc, _ = compress_rows(full[:, :, 1], cmp_w1[1], cmp_b1[1], cmp_w2[1], cmp_pos[1])
    kb = sel_blocks(full[:, :, 2])
    vb = sel_blocks(full[:, :, 3])
    wseq = jnp.concatenate([win_buf, win], axis=1)
    n_buf = win_buf.shape[1]
    w_pos = n_past - n_buf + jnp.arange(n_buf + T, dtype=jnp.int32)
    o = nsa_attend(q, q_rot, gate, kc, vc, c_end, kb, vb, wseq[:, :, 0], wseq[:, :, 1], w_pos, pos)
    y = x + (o.reshape(B, T, NSA_INNER) * jax.nn.silu(z)) @ w_out
    keep = min(NSA_WINDOW, n_buf + T)
    return y, rows, wseq[:, n_buf + T - keep:]


def dil_project(h, w_in, pos):
    B, T, _ = h.shape
    parts = _split(h @ w_in, [DIL_INNER] * (3 * N_DIL_GROUPS + 1))
    heads = lambda a: a.reshape(B, T, DIL_HEADS, HEAD_DIM)
    qs, kvs = [], []
    for gi in range(N_DIL_GROUPS):
        q, k, v = parts[3 * gi:3 * gi + 3]
        qs.append(partial_rope(heads(q), pos))
        kvs.append(jnp.stack([partial_rope(heads(k), pos), heads(v)], axis=2))
    return qs, kvs, parts[-1]


def dil_attend(q, kseq, vseq, q_local, q_abs, dil, n_back):
    offs = jnp.arange(n_back + 1, dtype=jnp.int32) * dil
    idx = jnp.clip(q_local[:, None] - offs[None, :], 0, kseq.shape[1] - 1)
    ok = (q_abs[:, None] - offs[None, :]) >= 0
    kk = kseq[:, idx]
    vv = vseq[:, idx]
    s = jnp.einsum('bqhd,bqkhd->bqhk', q, kk).astype(jnp.float32) * (HEAD_DIM ** -0.5)
    p, m, den = masked_softmax(s, ok[None, :, None, :], -1)
    o = jnp.einsum('bqhk,bqkhd->bqhd', p.astype(vv.dtype), vv)
    lse = (m + jnp.log(den))[..., 0]
    return o, lse


def dil_merge(outs, lses):
    a = jax.nn.softmax(jnp.stack(lses, axis=0), axis=0)
    o = sum(a[i][..., None] * outs[i].astype(jnp.float32) for i in range(len(outs)))
    return o.astype(outs[0].dtype)


def dil_prompt_layer(x, g, w_in, w_out):
    B, T, _ = x.shape
    pos = jnp.arange(T, dtype=jnp.int32)
    qs, kvs, z = dil_project(rms_norm(x, g), w_in, pos)
    pads = [jnp.pad(kv, ((0, 0), (w, 0), (0, 0), (0, 0), (0, 0))) for kv, (w, _) in zip(kvs, DIL_PAIRS)]

    def q_block(i):
        s = i * Q_BLOCK
        local = jnp.arange(Q_BLOCK, dtype=jnp.int32)
        outs, lses = [], []
        for q, kvp, (w, d) in zip(qs, pads, DIL_PAIRS):
            seg = lax.dynamic_slice_in_dim(kvp, s, w + Q_BLOCK, axis=1)
            o, lse = dil_attend(lax.dynamic_slice_in_dim(q, s, Q_BLOCK, axis=1),
                                seg[:, :, 0], seg[:, :, 1], local + w, s + local, d, w // d)
            outs.append(o)
            lses.append(lse)
        return dil_merge(outs, lses)

    o = lax.map(q_block, jnp.arange(T // Q_BLOCK, dtype=jnp.int32))
    o = jnp.moveaxis(o, 0, 1).reshape(B, T, DIL_INNER)
    y = x + (o * jax.nn.silu(z)) @ w_out
    new = [kv[:, T - min(w, T):] for kv, (w, _) in zip(kvs, DIL_PAIRS)]
    return y, new


def dil_sample_layer(x, bufs, n_past, g, w_in, w_out):
    B, T, _ = x.shape
    pos = n_past + jnp.arange(T, dtype=jnp.int32)
    qs, kvs, z = dil_project(rms_norm(x, g), w_in, pos)
    outs, lses, new = [], [], []
    for q, kv, buf, (w, d) in zip(qs, kvs, bufs, DIL_PAIRS):
        seq = jnp.concatenate([buf, kv], axis=1)
        n_buf = buf.shape[1]
        o, lse = dil_attend(q, seq[:, :, 0], seq[:, :, 1],
                            n_buf + jnp.arange(T, dtype=jnp.int32), pos, d, w // d)
        outs.append(o)
        lses.append(lse)
        keep = min(w, n_buf + T)
        new.append(seq[:, n_buf + T - keep:])
    o = dil_merge(outs, lses).reshape(B, T, DIL_INNER)
    y = x + (o * jax.nn.silu(z)) @ w_out
    return y, new


def setup_inputs(seed: int = 0) -> dict:
    key = jax.random.key(seed)
    k = jax.random.split(key, 20)
    f32 = jnp.float32
    nrm = lambda kk, shape, scale: scale * jax.random.normal(kk, shape, f32)
    n_pages = PAST_LEN // PAGE_SIZE
    n_used = DEC_BATCH * n_pages
    n_pool = n_used + max(1, n_used // 4)
    page_table = jax.random.permutation(k[0], n_pool)[:n_used].reshape(DEC_BATCH, n_pages).astype(jnp.int32)
    dil_states = [nrm(k[5 + j], (N_DIL_LAYERS, DEC_BATCH, min(w, PAST_LEN), 2, DIL_HEADS, HEAD_DIM), 1.0)
                  for j, (w, _) in enumerate(DIL_PAIRS)]
    return {
        'x_prompt': nrm(k[1], (BATCH, SEQ, D_MODEL), 1.0),
        'x_sample': nrm(k[2], (DEC_BATCH, DEC_SEQ, D_MODEL), 1.0),
        'cache_nsa_kv': nrm(k[3], (N_NSA_LAYERS, n_pool, PAGE_SIZE, N_PAGED_STREAMS, NSA_KV_HEADS, HEAD_DIM), 1.0),
        'state_nsa_win': nrm(k[4], (N_NSA_LAYERS, DEC_BATCH, min(NSA_WINDOW, PAST_LEN), 2, NSA_KV_HEADS, HEAD_DIM), 1.0),
        'state_dil_g0': dil_states[0],
        'state_dil_g1': dil_states[1],
        'state_dil_g2': dil_states[2],
        'page_table': page_table,
        'nsa_norm': 1.0 + nrm(k[8], (N_NSA_LAYERS, D_MODEL), 0.02),
        'nsa_w_in': nrm(k[9], (N_NSA_LAYERS, D_MODEL, NSA_IN_WIDTH), D_MODEL ** -0.5),
        'nsa_cmp_w1': nrm(k[10], (N_NSA_LAYERS, 2, CMP_BLOCK * HEAD_DIM, CMP_HIDDEN), (CMP_BLOCK * HEAD_DIM) ** -0.5),
        'nsa_cmp_b1': nrm(k[11], (N_NSA_LAYERS, 2, CMP_HIDDEN), 0.02),
        'nsa_cmp_w2': nrm(k[12], (N_NSA_LAYERS, 2, CMP_HIDDEN, HEAD_DIM), CMP_HIDDEN ** -0.5),
        'nsa_cmp_pos': nrm(k[13], (N_NSA_LAYERS, 2, CMP_BLOCK, HEAD_DIM), 0.1),
        'nsa_w_out': nrm(k[14], (N_NSA_LAYERS, NSA_INNER, D_MODEL), NSA_INNER ** -0.5),
        'dil_norm': 1.0 + nrm(k[15], (N_DIL_LAYERS, D_MODEL), 0.02),
        'dil_w_in': nrm(k[16], (N_DIL_LAYERS, D_MODEL, DIL_IN_WIDTH), D_MODEL ** -0.5),
        'dil_w_out': nrm(k[17], (N_DIL_LAYERS, DIL_INNER, D_MODEL), DIL_INNER ** -0.5),
        'final_norm': 1.0 + nrm(k[18], (D_MODEL,), 0.02),
    }


def reference(x_prompt, x_sample, cache_nsa_kv, state_nsa_win, state_dil_g0, state_dil_g1, state_dil_g2,
              page_table, nsa_norm, nsa_w_in, nsa_cmp_w1, nsa_cmp_b1, nsa_cmp_w2, nsa_cmp_pos, nsa_w_out,
              dil_norm, dil_w_in, dil_w_out, final_norm):
    n_past = page_table.shape[1] * PAGE_SIZE
    dil_states = (state_dil_g0, state_dil_g1, state_dil_g2)
    y_p, y_s = x_prompt, x_sample
    nsa_rows_p, nsa_rows_s, nsa_win_p, nsa_win_s = [], [], [], []
    dil_p = [[] for _ in DIL_PAIRS]
    dil_s = [[] for _ in DIL_PAIRS]
    for layer in range(DEPTH):
        li = layer // 2
        if layer % 2 == 0:
            cmp = (nsa_cmp_w1[li], nsa_cmp_b1[li], nsa_cmp_w2[li], nsa_cmp_pos[li])
            y_p, rows_p, win_p = nsa_prompt_layer(y_p, nsa_norm[li], nsa_w_in[li], *cmp, nsa_w_out[li])
            y_s, rows_s, win_s = nsa_sample_layer(y_s, cache_nsa_kv, li, page_table, state_nsa_win[li],
                                                  nsa_norm[li], nsa_w_in[li], *cmp, nsa_w_out[li])
            nsa_rows_p.append(rows_p)
            nsa_rows_s.append(rows_s)
            nsa_win_p.append(win_p)
            nsa_win_s.append(win_s)
        else:
            y_p, new_p = dil_prompt_layer(y_p, dil_norm[li], dil_w_in[li], dil_w_out[li])
            y_s, new_s = dil_sample_layer(y_s, [st[li] for st in dil_states], n_past,
                                          dil_norm[li], dil_w_in[li], dil_w_out[li])
            for j in range(N_DIL_GROUPS):
                dil_p[j].append(new_p[j])
                dil_s[j].append(new_s[j])
    y_p = rms_norm(y_p, final_norm)
    y_s = rms_norm(y_s, final_norm)
    return (y_p, y_s, jnp.stack(nsa_rows_p), jnp.stack(nsa_rows_s), jnp.stack(nsa_win_p), jnp.stack(nsa_win_s),
            jnp.stack(dil_p[0]), jnp.stack(dil_s[0]), jnp.stack(dil_p[1]), jnp.stack(dil_s[1]),
            jnp.stack(dil_p[2]), jnp.stack(dil_s[2]))
```

```python
import functools

import jax
import jax.numpy as jnp
from jax import lax
from jax.experimental import pallas as pl
from jax.experimental.pallas import tpu as pltpu

F32 = jnp.float32
BF16 = jnp.bfloat16

D_MODEL = 1024
HEAD_DIM = 64
ROT_DIM = HEAD_DIM // 4
ROPE_THETA = 500000.0
NORM_EPS = 1e-6
NEG_INF = -1e30
Q_SCALE = HEAD_DIM ** -0.5
PAGE_SIZE = 128

NSA_KV_HEADS = 4
NSA_GROUP = 4
NSA_KV_WIDTH = NSA_KV_HEADS * HEAD_DIM
CMP_BLOCK = 32
CMP_STRIDE = 16
SEL_BLOCK = 64
SEL_TOP_N = 16
NSA_WINDOW = 512
FORCE_BONUS = 1e4
DIL_PAIRS = ((128, 1), (512, 4), (2048, 16))
DIL_HEADS = 16

LANES = 128
Q_TILE = 128
SEL_TILE = 256
VMEM_LIMIT = 56 * 1024 * 1024

_NT = (((1,), (1,)), ((), ()))


def _cparams(*sem):
    return pltpu.CompilerParams(dimension_semantics=sem, vmem_limit_bytes=VMEM_LIMIT)


def _rope_tables(pos):
    half = ROT_DIM // 2
    inv = ROPE_THETA ** (-jnp.arange(half, dtype=F32) / half)
    ang = pos.astype(F32)[:, None] * inv[None, :]
    cos, sin = jnp.cos(ang), jnp.sin(ang)
    m = pos.shape[0]
    one, zero, z8 = jnp.ones((m, 48), F32), jnp.zeros((m, 48), F32), jnp.zeros((m, 8), F32)
    c = jnp.concatenate([cos, cos, one], axis=1)
    s1 = jnp.concatenate([-sin, z8, zero], axis=1)
    s2 = jnp.concatenate([z8, sin, zero], axis=1)
    return jnp.concatenate([c, c, s1, s1, s2, s2], axis=1)


def _rope(y, tab_ref):
    c, s1, s2 = tab_ref[:, 0:128], tab_ref[:, 128:256], tab_ref[:, 256:384]
    out = []
    for j in range(y.shape[1] // LANES):
        yc = y[:, j * LANES:(j + 1) * LANES]
        out.append(yc * c + pltpu.roll(yc, LANES - 8, 1) * s1 + pltpu.roll(yc, 8, 1) * s2)
    return out[0] if len(out) == 1 else jnp.concatenate(out, axis=1)


def _normed_bf16(x_ref, g_ref):
    x = x_ref[...]
    ms = jnp.mean(x * x, axis=-1, keepdims=True)
    return ((x * lax.rsqrt(ms + NORM_EPS)) * g_ref[...]).astype(BF16)


def _silu(z):
    return z * jax.nn.sigmoid(z)


def _nsa_proj_kernel(x_ref, g_ref, w_ref, tab_ref, qraw_ref, qrot_ref, rows_ref, rowsb_ref,
                     win_ref, winb_ref, sz_ref, gate_ref):
    hb = _normed_bf16(x_ref, g_ref)
    dot = lambda a, b: jnp.dot(hb, w_ref[:, a:b], preferred_element_type=F32)
    q = dot(0, 1024) * Q_SCALE
    qraw_ref[...] = q.astype(BF16)
    qrot_ref[...] = _rope(q, tab_ref).astype(BF16)
    kv = dot(1024, 2048)
    rows = jnp.concatenate([kv[:, :512], _rope(kv[:, 512:768], tab_ref), kv[:, 768:]], axis=1)
    rows_ref[...] = rows
    rowsb_ref[...] = rows.astype(BF16)
    wkv = dot(2048, 2560)
    win = jnp.concatenate([_rope(wkv[:, :256], tab_ref), wkv[:, 256:]], axis=1)
    win_ref[...] = win
    winb_ref[...] = win.astype(BF16)
    sz_ref[...] = _silu(dot(2560, 3584))
    gate_ref[...] = jax.nn.sigmoid(dot(3584, 3712))


def _nsa_project(x2, g, w_in, pos, tm):
    m = x2.shape[0]
    q, kv, wkv, gl, z = (w_in[:, :1024], w_in[:, 1024:2048], w_in[:, 2048:2560],
                         w_in[:, 2560:2608], w_in[:, 2608:])
    w = jnp.concatenate([q, kv, wkv, z, gl, jnp.zeros((D_MODEL, 80), F32)], axis=1).astype(BF16)
    tab = _rope_tables(pos)
    row = lambda n: pl.BlockSpec((tm, n), lambda i: (i, 0))
    full = lambda a: pl.BlockSpec(a.shape, lambda i: (0, 0))
    g2 = g.reshape(1, D_MODEL)
    shapes = [((m, 1024), BF16), ((m, 1024), BF16), ((m, 1024), F32), ((m, 1024), BF16),
              ((m, 512), F32), ((m, 512), BF16), ((m, 1024), F32), ((m, 128), F32)]
    return pl.pallas_call(
        _nsa_proj_kernel,
        grid=(m // tm,),
        in_specs=[row(D_MODEL), full(g2), full(w), row(384)],
        out_specs=[row(s[1]) for s, _ in shapes],
        out_shape=[jax.ShapeDtypeStruct(s, d) for s, d in shapes],
        compiler_params=_cparams("parallel"),
        name="nsa_proj",
    )(x2, g2, w, tab)


def _dil_proj_kernel(x_ref, g_ref, w_ref, tab_ref, q_ref, kv_ref, kvb_ref, *sz_ref):
    hb = _normed_bf16(x_ref, g_ref)
    dot = lambda a, b: jnp.dot(hb, w_ref[:, a:b], preferred_element_type=F32)
    q_ref[...] = _rope(dot(0, 1024) * Q_SCALE, tab_ref).astype(BF16)
    k = _rope(dot(1024, 2048), tab_ref)
    v = dot(2048, 3072)
    kv_ref[:, :1024] = k
    kv_ref[:, 1024:] = v
    kvb_ref[:, :1024] = k.astype(BF16)
    kvb_ref[:, 1024:] = v.astype(BF16)
    if sz_ref:
        sz_ref[0][...] = _silu(dot(3072, 4096))


def _dil_project(x2, g, w_in, pos, tm):
    m = x2.shape[0]
    tab = _rope_tables(pos)
    g2 = g.reshape(1, D_MODEL)
    row = lambda n: pl.BlockSpec((tm, n), lambda i: (i, 0))
    full = lambda a: pl.BlockSpec(a.shape, lambda i: (0, 0))
    outs = []
    for gi in range(len(DIL_PAIRS)):
        last = gi == len(DIL_PAIRS) - 1
        cols = [w_in[:, 3072 * gi:3072 * (gi + 1)]] + ([w_in[:, 9216:]] if last else [])
        w = jnp.concatenate(cols, axis=1).astype(BF16)
        shapes = [((m, 1024), BF16), ((m, 2048), F32), ((m, 2048), BF16)] + ([((m, 1024), F32)] if last else [])
        outs.append(pl.pallas_call(
            _dil_proj_kernel,
            grid=(m // tm,),
            in_specs=[row(D_MODEL), full(g2), full(w), row(384)],
            out_specs=[row(s[1]) for s, _ in shapes],
            out_shape=[jax.ShapeDtypeStruct(s, d) for s, d in shapes],
            compiler_params=_cparams("parallel"),
            name=f"dil_proj_g{gi}",
        )(x2, g2, w, tab))
    return outs


def _finish(y, gf_ref):
    if gf_ref is None:
        return y
    ms = jnp.mean(y * y, axis=-1, keepdims=True)
    return (y * lax.rsqrt(ms + NORM_EPS)) * gf_ref[...]


def _nsa_out_kernel(o_ref, sz_ref, x_ref, w_ref, y_ref):
    a = (o_ref[...] * sz_ref[...]).astype(BF16)
    y_ref[...] = x_ref[...] + jnp.dot(a, w_ref[...], preferred_element_type=F32)


def _dil_out_kernel(o3_ref, l3_ref, sz_ref, x_ref, w_ref, gf_ref, y_ref):
    l0, l1, l2 = l3_ref[0], l3_ref[1], l3_ref[2]
    m = jnp.maximum(jnp.maximum(l0, l1), l2)
    e0, e1, e2 = jnp.exp(l0 - m), jnp.exp(l1 - m), jnp.exp(l2 - m)
    inv = 1.0 / (e0 + e1 + e2)
    o = (e0 * inv) * o3_ref[0] + (e1 * inv) * o3_ref[1] + (e2 * inv) * o3_ref[2]
    a = (o * sz_ref[...]).astype(BF16)
    y = x_ref[...] + jnp.dot(a, w_ref[...], preferred_element_type=F32)
    y_ref[...] = _finish(y, gf_ref)


def _nsa_out(o, sz, x2, w_out, tm):
    m = x2.shape[0]
    row = pl.BlockSpec((tm, D_MODEL), lambda i: (i, 0))
    w = w_out.astype(BF16)
    return pl.pallas_call(
        _nsa_out_kernel, grid=(m // tm,),
        in_specs=[row, row, row, pl.BlockSpec(w.shape, lambda i: (0, 0))],
        out_specs=row, out_shape=jax.ShapeDtypeStruct((m, D_MODEL), F32),
        compiler_params=_cparams("parallel"), name="nsa_out",
    )(o, sz, x2, w)


def _dil_out(o3, l3, sz, x2, w_out, gf, tm):
    m = x2.shape[0]
    row = pl.BlockSpec((tm, D_MODEL), lambda i: (i, 0))
    row3 = pl.BlockSpec((3, tm, D_MODEL), lambda i: (0, i, 0))
    w = w_out.astype(BF16)
    gf2 = gf.reshape(1, D_MODEL)
    return pl.pallas_call(
        _dil_out_kernel, grid=(m // tm,),
        in_specs=[row3, row3, row, row, pl.BlockSpec(w.shape, lambda i: (0, 0)),
                  pl.BlockSpec(gf2.shape, lambda i: (0, 0))],
        out_specs=row, out_shape=jax.ShapeDtypeStruct((m, D_MODEL), F32),
        compiler_params=_cparams("parallel"), name="dil_out",
    )(o3, l3, sz, x2, w, gf2)


def _dil_prompt_kernel(q_ref, kp_ref, kc_ref, vp_ref, vc_ref, o_ref, lse_ref):
    ib = pl.program_id(1)
    tq = lax.broadcasted_iota(jnp.int32, (Q_TILE, 2 * Q_TILE), 0)
    kk = lax.broadcasted_iota(jnp.int32, (Q_TILE, 2 * Q_TILE), 1)
    diff = kk - tq
    lo_key = jnp.where(ib > 0, 0, Q_TILE)
    mask = jnp.logical_and(jnp.logical_and(diff >= 0, diff <= Q_TILE), kk >= lo_key)
    lo = lax.broadcasted_iota(jnp.int32, (Q_TILE, LANES), 1) < HEAD_DIM
    for pr in range(DIL_HEADS // 2):
        cs = slice(pr * LANES, (pr + 1) * LANES)
        q2 = q_ref[:, cs]
        k2 = jnp.concatenate([kp_ref[:, cs], kc_ref[:, cs]], axis=0)
        v2 = jnp.concatenate([vp_ref[:, cs], vc_ref[:, cs]], axis=0)
        outs, lses = [], []
        for half in range(2):
            keep = lo if half == 0 else jnp.logical_not(lo)
            qm = jnp.where(keep, q2, jnp.zeros_like(q2))
            s = lax.dot_general(qm, k2, _NT, preferred_element_type=F32)
            s = jnp.where(mask, s, NEG_INF)
            m = jnp.max(s, axis=-1, keepdims=True)
            e = jnp.where(mask, jnp.exp(s - m), 0.0)
            den = jnp.sum(e, axis=-1, keepdims=True)
            p = e * (1.0 / jnp.maximum(den, 1e-30))
            outs.append(jnp.dot(p.astype(BF16), v2, preferred_element_type=F32))
            lses.append(jnp.broadcast_to(m + jnp.log(den), (Q_TILE, LANES)))
        o_ref[:, cs] = jnp.where(lo, outs[0], outs[1])
        lse_ref[:, cs] = jnp.where(lo, lses[0], lses[1])


def _dil_prompt_attend(q, kvb, d):
    t = q.shape[0]
    n = t // d
    qv = q.reshape(n, d * 1024)
    kvv = kvb.reshape(n, d * 2048)
    blk = lambda f: pl.BlockSpec((Q_TILE, 1024), f)
    prev = lambda i: jnp.maximum(i - 1, 0)
    out = pl.pallas_call(
        _dil_prompt_kernel,
        grid=(d, n // Q_TILE),
        in_specs=[blk(lambda r, i: (i, r)),
                  blk(lambda r, i: (prev(i), 2 * r)), blk(lambda r, i: (i, 2 * r)),
                  blk(lambda r, i: (prev(i), 2 * r + 1)), blk(lambda r, i: (i, 2 * r + 1))],
        out_specs=[blk(lambda r, i: (i, r)), blk(lambda r, i: (i, r))],
        out_shape=[jax.ShapeDtypeStruct((n, d * 1024), F32)] * 2,
        compiler_params=_cparams("parallel", "arbitrary"),
        name=f"dil_prompt_d{d}",
    )(qv, kvv, kvv, kvv, kvv)
    return out[0].reshape(t, 1024), out[1].reshape(t, 1024)


def _dil_sample_kernel(st_ref, new_ref, qbd_ref, o_ref, lse_ref, newpad_ref, *, w, d):
    nt = new_ref.shape[1]
    qbd = qbd_ref[0]
    newpad_ref[...] = jnp.zeros_like(newpad_ref)
    newpad_ref[0:nt, :] = new_ref[0]
    t_lane = lax.broadcasted_iota(jnp.int32, (1, LANES), 1) % nt
    ks = st_ref[0, :, 0:1024].astype(BF16)
    s_st = jnp.dot(ks, qbd, preferred_element_type=F32)
    kn = newpad_ref[:, 0:1024].astype(BF16)
    s_nw = jnp.dot(kn, qbd, preferred_element_type=F32)
    j_st = lax.broadcasted_iota(jnp.int32, (w, LANES), 0)
    diff_st = (w + t_lane) - j_st
    m_st = jnp.logical_and((diff_st & (d - 1)) == 0, diff_st <= w)
    i_nw = lax.broadcasted_iota(jnp.int32, (LANES, LANES), 0)
    diff_nw = t_lane - i_nw
    m_nw = jnp.logical_and(jnp.logical_and(diff_nw >= 0, (diff_nw & (d - 1)) == 0), i_nw < nt)
    s_st = jnp.where(m_st, s_st, NEG_INF)
    s_nw = jnp.where(m_nw, s_nw, NEG_INF)
    m = jnp.maximum(jnp.max(s_st, axis=0, keepdims=True), jnp.max(s_nw, axis=0, keepdims=True))
    e_st = jnp.where(m_st, jnp.exp(s_st - m), 0.0)
    e_nw = jnp.where(m_nw, jnp.exp(s_nw - m), 0.0)
    den = jnp.sum(e_st, axis=0, keepdims=True) + jnp.sum(e_nw, axis=0, keepdims=True)
    inv = 1.0 / jnp.maximum(den, 1e-30)
    p_st = e_st * inv
    p_nw = e_nw * inv
    acc = jnp.dot(p_nw.T.astype(BF16), newpad_ref[:, 1024:2048].astype(BF16),
                  preferred_element_type=F32)
    for c in range(w // LANES):
        rs = slice(c * LANES, (c + 1) * LANES)
        acc = acc + jnp.dot(p_st[rs].T.astype(BF16), st_ref[0, rs, 1024:2048].astype(BF16),
                            preferred_element_type=F32)
    lse_col = jnp.broadcast_to(m + jnp.log(den), (LANES, LANES)).T
    lse_full = jnp.concatenate([lse_col] * (1024 // LANES), axis=1)
    hc = lax.broadcasted_iota(jnp.int32, (LANES, 1024), 0) // nt
    hv = lax.broadcasted_iota(jnp.int32, (LANES, 1024), 1) // HEAD_DIM
    diag = hc == hv

    def fold(a):
        a = jnp.where(diag, a, 0.0)
        out = a[0:nt]
        for h in range(1, DIL_HEADS):
            out = out + a[h * nt:(h + 1) * nt]
        return out

    o_ref[0] = fold(acc)
    lse_ref[0] = fold(lse_full)


def _dil_sample_attend(state, new_kv, qbd, w, d):
    b, nt = new_kv.shape[0], new_kv.shape[1]
    return pl.pallas_call(
        functools.partial(_dil_sample_kernel, w=w, d=d),
        grid=(b,),
        in_specs=[pl.BlockSpec((1, w, 2048), lambda i: (i, 0, 0)),
                  pl.BlockSpec((1, nt, 2048), lambda i: (i, 0, 0)),
                  pl.BlockSpec((1, 1024, LANES), lambda i: (i, 0, 0))],
        out_specs=[pl.BlockSpec((1, nt, 1024), lambda i: (i, 0, 0))] * 2,
        out_shape=[jax.ShapeDtypeStruct((b, nt, 1024), F32)] * 2,
        scratch_shapes=[pltpu.VMEM((LANES, 2048), F32)],
        compiler_params=_cparams("parallel"),
        name=f"dil_sample_d{d}",
    )(state, new_kv, qbd)


CMP_PAGES = 16
CHUNKS_PER_PAGE = PAGE_SIZE // CMP_STRIDE


def _compress_kernel(tbl_ref, *refs):
    del tbl_ref
    pages, nxt_ref = refs[:CMP_PAGES], refs[CMP_PAGES]
    w1_ref, pre_ref, w2_ref, out_ref, shift_ref = refs[CMP_PAGES + 1:]
    nch = CMP_PAGES * CHUNKS_PER_PAGE
    acc = [[None, None], [None, None]]
    for s in range(CMP_STRIDE):
        rows = [p[0, :, s, :] for p in pages]
        rows.append(jnp.broadcast_to(nxt_ref[0, 0, pl.ds(s, 1), :], (16, 512)))
        x = jnp.concatenate(rows, axis=0).astype(BF16)
        for st in range(2):
            xs = x[:, st * 256:(st + 1) * 256]
            for r in range(2):
                part = jnp.dot(xs, w1_ref[st, r, s], preferred_element_type=F32)
                acc[st][r] = part if acc[st][r] is None else acc[st][r] + part
    outs = []
    for st in range(2):
        shift_ref[...] = acc[st][1]
        pre = acc[st][0][0:nch] + shift_ref[pl.ds(1, nch), :] + pre_ref[:, st * 256:(st + 1) * 256]
        outs.append(jnp.dot(_silu(pre).astype(BF16), w2_ref[st], preferred_element_type=F32))
    out_ref[0] = jnp.concatenate(outs, axis=1)


def _blockdiag4(a):
    eye = jnp.eye(NSA_KV_HEADS, dtype=a.dtype)
    out = a[..., None, :, None, :] * eye[:, None, :, None]
    return out.reshape(a.shape[:-2] + (NSA_KV_HEADS * a.shape[-2], NSA_KV_HEADS * a.shape[-1]))


def _cmp_bias_kernel(pos_ref, w1_ref, b1_ref, out_ref):
    for st in range(2):
        pos = jnp.broadcast_to(pos_ref[st], (8, CMP_BLOCK * HEAD_DIM)).astype(BF16)
        out_ref[st] = b1_ref[st] + jnp.dot(pos, w1_ref[st].astype(BF16), preferred_element_type=F32)


def _compress(src, table, cmp_w1, cmp_b1, cmp_w2, cmp_pos):
    b, npg = table.shape
    steps = npg // CMP_PAGES
    r = CMP_BLOCK // CMP_STRIDE
    w1r = cmp_w1.reshape(2, r, CMP_STRIDE, HEAD_DIM, HEAD_DIM)
    w1bd = _blockdiag4(w1r).astype(BF16)
    w2bd = _blockdiag4(cmp_w2).astype(BF16)
    pre = pl.pallas_call(
        _cmp_bias_kernel,
        out_shape=jax.ShapeDtypeStruct((2, 8, HEAD_DIM), F32), name="cmp_bias",
    )(cmp_pos.reshape(2, 1, CMP_BLOCK * HEAD_DIM), cmp_w1, cmp_b1.reshape(2, 1, HEAD_DIM))
    pre = jnp.tile(pre[:, 0, :], (1, NSA_KV_HEADS)).reshape(1, 512)
    flat = table.reshape(-1)
    src = src.reshape(src.shape[0], CHUNKS_PER_PAGE, CMP_STRIDE, 1024)

    def page_spec(j):
        return pl.BlockSpec((1, CHUNKS_PER_PAGE, CMP_STRIDE, 512),
                            lambda bi, si, tbl: (tbl[bi * npg + si * CMP_PAGES + j], 0, 0, 0))

    nxt_spec = pl.BlockSpec(
        (1, 1, CMP_STRIDE, 512),
        lambda bi, si, tbl: (tbl[bi * npg + jnp.minimum((si + 1) * CMP_PAGES, npg - 1)], 0, 0, 0))
    const = lambda a: pl.BlockSpec(a.shape, lambda bi, si, tbl: (0,) * a.ndim)
    nch = CMP_PAGES * CHUNKS_PER_PAGE
    return pl.pallas_call(
        _compress_kernel,
        grid_spec=pltpu.PrefetchScalarGridSpec(
            num_scalar_prefetch=1, grid=(b, steps),
            in_specs=[page_spec(j) for j in range(CMP_PAGES)] + [nxt_spec, const(w1bd), const(pre), const(w2bd)],
            out_specs=pl.BlockSpec((1, nch, 512), lambda bi, si, tbl: (bi, si, 0)),
            scratch_shapes=[pltpu.VMEM((nch + 16, 256), F32)]),
        out_shape=jax.ShapeDtypeStruct((b, npg * CHUNKS_PER_PAGE, 512), F32),
        compiler_params=_cparams("parallel", "parallel"),
        name="nsa_compress",
    )(flat, *([src] * (CMP_PAGES + 1)), w1bd, pre, w2bd)


def _select_bias(imp_ref, n_blk, n_rows, q_pos):
    cols = q_pos.shape[1]
    imp_s = imp_ref[pl.ds(7, n_rows, stride=4), :]
    for e in range(1, 5):
        imp_s = imp_s + imp_ref[pl.ds(7 + e, n_rows, stride=4), :]
    blk = lax.broadcasted_iota(jnp.int32, (n_rows, cols), 0)
    cur = q_pos // SEL_BLOCK
    valid = blk * SEL_BLOCK <= q_pos
    forced = jnp.logical_or(jnp.logical_or(blk == 0, blk == cur), blk == cur - 1)
    score = jnp.where(valid, imp_s, -1.0) + jnp.where(forced, FORCE_BONUS, 0.0)
    work = jnp.where(blk < n_blk, score, -jnp.inf)
    bias = jnp.full((n_rows, cols), NEG_INF, F32)
    for _ in range(min(SEL_TOP_N, n_blk)):
        mx = jnp.max(work, axis=0, keepdims=True)
        first = jnp.min(jnp.where(work == mx, blk, n_rows), axis=0, keepdims=True)
        pick = blk == first
        bias = jnp.where(pick, 0.0, bias)
        work = jnp.where(pick, -jnp.inf, work)
    return bias


def _softmax_cols(s, mask):
    s = jnp.where(mask, s, NEG_INF)
    m = jnp.max(s, axis=0, keepdims=True)
    e = jnp.where(mask, jnp.exp(s - m), 0.0)
    den = jnp.sum(e, axis=0, keepdims=True)
    return e * (1.0 / jnp.maximum(den, 1e-30))


def _nsa_prompt_kernel(qraw_ref, qrot_ref, kc_ref, vct_ref, ks_ref, vst_ref, kw_ref, vwt_ref,
                       gate_ref, o_ref, imp_ref, bias_ref, *, n_cmp_pad, n_sel):
    qb = pl.program_id(1)
    s0 = qb * Q_TILE
    cols = NSA_GROUP * Q_TILE

    q_raw, q_rot = qraw_ref[0, 0], qrot_ref[0, 0]
    t_col = lax.broadcasted_iota(jnp.int32, (1, cols), 1) % Q_TILE
    q_pos = s0 + t_col

    s_c = lax.dot_general(kc_ref[0], q_raw, _NT, preferred_element_type=F32)
    c_end = lax.broadcasted_iota(jnp.int32, (n_cmp_pad, 1), 0) * CMP_STRIDE + (CMP_BLOCK - 1)
    p_c = _softmax_cols(s_c, c_end <= q_pos)
    o_c = jnp.dot(vct_ref[0], p_c.astype(BF16), preferred_element_type=F32)

    imp = p_c[:, 0:Q_TILE]
    for r in range(1, NSA_GROUP):
        imp = imp + p_c[:, r * Q_TILE:(r + 1) * Q_TILE]
    imp_ref[0:8, :] = jnp.zeros((8, Q_TILE), F32)
    imp_ref[8:8 + n_cmp_pad, :] = imp
    imp_ref[8 + n_cmp_pad:16 + n_cmp_pad, :] = jnp.zeros((8, Q_TILE), F32)
    bias_ref[...] = _select_bias(imp_ref, n_sel, n_sel, q_pos[:, 0:Q_TILE])

    def tile_scores(kt):
        k0 = pl.multiple_of(kt * SEL_TILE, SEL_TILE)
        s = lax.dot_general(ks_ref[0, pl.ds(k0, SEL_TILE), :], q_rot, _NT,
                            preferred_element_type=F32)
        rows = []
        for i in range(SEL_TILE // SEL_BLOCK):
            b = bias_ref[pl.ds(kt * (SEL_TILE // SEL_BLOCK) + i, 1), :]
            rows.append(jnp.broadcast_to(b, (SEL_BLOCK, Q_TILE)))
        btile = jnp.concatenate(rows, axis=0)
        return s + jnp.concatenate([btile] * NSA_GROUP, axis=1)

    def update(kt, s, carry):
        m, l, acc = carry
        m_new = jnp.maximum(m, jnp.max(s, axis=0, keepdims=True))
        alpha = jnp.exp(m - m_new)
        p = jnp.exp(s - m_new)
        l = alpha * l + jnp.sum(p, axis=0, keepdims=True)
        acc = alpha * acc + jnp.dot(vst_ref[0, kt], p.astype(BF16), preferred_element_type=F32)
        return m_new, l, acc

    n_full = s0 // SEL_TILE
    carry = (jnp.full((1, cols), NEG_INF, F32), jnp.zeros((1, cols), F32), jnp.zeros((HEAD_DIM, cols), F32))
    carry = lax.fori_loop(0, n_full, lambda kt, c: update(kt, tile_scores(kt), c), carry)
    s_t = tile_scores(n_full)
    k_pos = n_full * SEL_TILE + lax.broadcasted_iota(jnp.int32, (SEL_TILE, 1), 0)
    s_t = jnp.where(k_pos <= q_pos, s_t, NEG_INF)
    _, l, acc = update(n_full, s_t, carry)
    o_s = acc * (1.0 / l)

    n_w = NSA_WINDOW + Q_TILE
    s_w = lax.dot_general(kw_ref[0, pl.ds(pl.multiple_of(s0, Q_TILE), n_w), :], q_rot, _NT,
                          preferred_element_type=F32)
    j = lax.broadcasted_iota(jnp.int32, (n_w, 1), 0)
    dpos = NSA_WINDOW + t_col - j
    m_w = jnp.logical_and(jnp.logical_and(dpos >= 0, dpos < NSA_WINDOW), j + s0 >= NSA_WINDOW)
    p_w = _softmax_cols(s_w, m_w).astype(BF16)
    o_w = jnp.zeros((HEAD_DIM, cols), F32)
    for i in range(n_w // Q_TILE):
        o_w = o_w + jnp.dot(vwt_ref[0, qb + i], p_w[i * Q_TILE:(i + 1) * Q_TILE],
                            preferred_element_type=F32)

    g = gate_ref[0, 0]
    o_t = g[0:1] * o_c + g[1:2] * o_s + g[2:3] * o_w
    for pr in range(NSA_GROUP // 2):
        two = jnp.concatenate([o_t[:, (2 * pr) * Q_TILE:(2 * pr + 1) * Q_TILE],
                               o_t[:, (2 * pr + 1) * Q_TILE:(2 * pr + 2) * Q_TILE]], axis=0)
        o_ref[:, pr * LANES:(pr + 1) * LANES] = two.T


def _nsa_prompt_attend(q_raw, q_rot, gate, kc, vc, rows_b, win_b):
    t = q_raw.shape[0]
    nq = t // Q_TILE
    n_cmp_pad = kc.shape[0]
    n_sel = t // SEL_BLOCK
    heads = lambda a: a.reshape(a.shape[0], NSA_KV_HEADS, HEAD_DIM).transpose(1, 0, 2)
    kc_h = heads(kc.astype(BF16))
    vc_t = heads(vc.astype(BF16)).transpose(0, 2, 1)
    ks_h = heads(rows_b[:, 512:768])
    vs_t = heads(rows_b[:, 768:1024]).reshape(NSA_KV_HEADS, t // SEL_TILE, SEL_TILE, HEAD_DIM)
    vs_t = vs_t.transpose(0, 1, 3, 2)
    winp = jnp.pad(win_b, ((NSA_WINDOW, 0), (0, 0)))
    tw = t + NSA_WINDOW
    kw_h = heads(winp[:, 0:256])
    vw_t = heads(winp[:, 256:512]).reshape(NSA_KV_HEADS, tw // Q_TILE, Q_TILE, HEAD_DIM)
    vw_t = vw_t.transpose(0, 1, 3, 2)
    gt = gate[:, :48].reshape(nq, Q_TILE, NSA_KV_HEADS, NSA_GROUP, 3).transpose(0, 2, 4, 3, 1)
    gt = gt.reshape(nq, NSA_KV_HEADS, 3, NSA_GROUP * Q_TILE)
    per_head = lambda a: pl.BlockSpec((1,) + a.shape[1:], lambda g, i: (g,) + (0,) * (a.ndim - 1))
    qspec = pl.BlockSpec((Q_TILE, NSA_GROUP * HEAD_DIM), lambda g, i: (i, g))

    def stacked(q):
        q5 = q.reshape(nq, Q_TILE, NSA_KV_HEADS, NSA_GROUP, HEAD_DIM).transpose(0, 2, 3, 1, 4)
        return q5.reshape(nq, NSA_KV_HEADS, NSA_GROUP * Q_TILE, HEAD_DIM)

    q_raw, q_rot = stacked(q_raw), stacked(q_rot)
    sspec = pl.BlockSpec((1, 1, NSA_GROUP * Q_TILE, HEAD_DIM), lambda g, i: (i, g, 0, 0))
    return pl.pallas_call(
        functools.partial(_nsa_prompt_kernel, n_cmp_pad=n_cmp_pad, n_sel=n_sel),
        grid=(NSA_KV_HEADS, nq),
        in_specs=[sspec, sspec, per_head(kc_h), per_head(vc_t), per_head(ks_h), per_head(vs_t),
                  per_head(kw_h), per_head(vw_t),
                  pl.BlockSpec((1, 1, 3, NSA_GROUP * Q_TILE), lambda g, i: (i, g, 0, 0))],
        out_specs=qspec,
        out_shape=jax.ShapeDtypeStruct((t, 1024), F32),
        scratch_shapes=[pltpu.VMEM((n_cmp_pad + 16, Q_TILE), F32), pltpu.VMEM((n_sel, Q_TILE), F32)],
        compiler_params=_cparams("parallel", "arbitrary"),
        name="nsa_prompt_attn",
    )(q_raw, q_rot, kc_h, vc_t, ks_h, vs_t, kw_h, vw_t, gt)


SMP_PAGES = 16


def _nsa_sample_kernel(tbl_ref, *refs, n_cmp_pad, n_sel, n_sel_pad, n_past):
    del tbl_ref
    pages = refs[:SMP_PAGES]
    (qraw_ref, qrot_ref, kc_ref, vc_ref, new_ref, wst_ref, wnew_ref, gate_ref, o_ref,
     imp_ref, bias_ref, m_ref, l_ref, acc_ref, oc_ref) = refs[SMP_PAGES:]
    si = pl.program_id(1)
    nt = new_ref.shape[1]
    t_col = lax.broadcasted_iota(jnp.int32, (1, LANES), 1) % nt
    q_pos = n_past + t_col
    q_rot = qrot_ref[0]

    def col_vec(row):
        return jnp.broadcast_to(row, (LANES, LANES)).T[:, 0:1]

    def pv(p_t, v):
        out = jnp.zeros((LANES, NSA_KV_WIDTH), F32)
        for c in range(p_t.shape[0] // LANES):
            rs = slice(c * LANES, (c + 1) * LANES)
            out = out + jnp.dot(p_t[rs].T.astype(BF16), v[rs], preferred_element_type=F32)
        return out

    @pl.when(si == 0)
    def _():
        s_c = lax.dot_general(kc_ref[0].astype(BF16), qraw_ref[0], _NT, preferred_element_type=F32)
        c_end = lax.broadcasted_iota(jnp.int32, (n_cmp_pad, 1), 0) * CMP_STRIDE + (CMP_BLOCK - 1)
        valid = jnp.logical_and(c_end <= q_pos, c_end < n_past)
        p_c = _softmax_cols(s_c, valid)
        oc_ref[...] = pv(p_c, vc_ref[0].astype(BF16))
        imp = p_c
        for r in range(1, NSA_GROUP):
            imp = imp + pltpu.roll(p_c, r * (LANES // NSA_GROUP), 1)
        imp_ref[0:8, :] = jnp.zeros((8, LANES), F32)
        imp_ref[8:8 + n_cmp_pad, :] = imp
        imp_ref[8 + n_cmp_pad:, :] = jnp.zeros((imp_ref.shape[0] - 8 - n_cmp_pad, LANES), F32)
        bias_ref[...] = _select_bias(imp_ref, n_sel, n_sel_pad, q_pos)
        m_ref[...] = jnp.full((1, LANES), NEG_INF, F32)
        l_ref[...] = jnp.zeros((1, LANES), F32)
        acc_ref[...] = jnp.zeros_like(acc_ref)

    def online(s, v):
        m_old = m_ref[...]
        m_new = jnp.maximum(m_old, jnp.max(s, axis=0, keepdims=True))
        alpha = jnp.exp(m_old - m_new)
        p = jnp.exp(s - m_new)
        l_ref[...] = alpha * l_ref[...] + jnp.sum(p, axis=0, keepdims=True)
        acc_ref[...] = col_vec(alpha) * acc_ref[...] + pv(p, v)
        m_ref[...] = m_new

    keys = jnp.concatenate([p[0, :, 0:256] for p in pages], axis=0).astype(BF16)
    vals = jnp.concatenate([p[0, :, 256:512] for p in pages], axis=0).astype(BF16)
    s = lax.dot_general(keys, q_rot, _NT, preferred_element_type=F32)
    per_step = SMP_PAGES * PAGE_SIZE // SEL_BLOCK
    rows = []
    for i in range(per_step):
        b = bias_ref[pl.ds(si * per_step + i, 1), :]
        rows.append(jnp.broadcast_to(b, (SEL_BLOCK, LANES)))
    online(s + jnp.concatenate(rows, axis=0), vals)

    @pl.when(si == pl.num_programs(1) - 1)
    def _():
        pad = jnp.zeros((LANES - nt, NSA_KV_WIDTH), F32)
        k_new = jnp.concatenate([new_ref[0, :, 512:768], pad], axis=0).astype(BF16)
        v_new = jnp.concatenate([new_ref[0, :, 768:1024], pad], axis=0).astype(BF16)
        s_n = lax.dot_general(k_new, q_rot, _NT, preferred_element_type=F32)
        i_row = lax.broadcasted_iota(jnp.int32, (LANES, 1), 0)
        ok = jnp.logical_and(i_row <= t_col, i_row < nt)
        s_n = s_n + bias_ref[pl.ds(n_past // SEL_BLOCK, 1), :]
        online(jnp.where(ok, s_n, NEG_INF), v_new)
        o_s = acc_ref[...] * col_vec(1.0 / l_ref[...])

        n_buf = wst_ref.shape[1]
        wk = jnp.concatenate([wst_ref[0, :, 0:256], wnew_ref[0, :, 0:256], pad], axis=0).astype(BF16)
        wv = jnp.concatenate([wst_ref[0, :, 256:512], wnew_ref[0, :, 256:512], pad], axis=0).astype(BF16)
        s_w = lax.dot_general(wk, q_rot, _NT, preferred_element_type=F32)
        j = lax.broadcasted_iota(jnp.int32, (n_buf + LANES, 1), 0)
        dpos = n_buf + t_col - j
        m_w = jnp.logical_and(jnp.logical_and(dpos >= 0, dpos < NSA_WINDOW), j < n_buf + nt)
        o_w = pv(_softmax_cols(s_w, m_w), wv)

        g = gate_ref[0]
        o_ref[0] = g[:, 0:1] * oc_ref[...] + g[:, 1:2] * o_s + g[:, 2:3] * o_w


def _nsa_sample_attend(q_raw, q_rot, gate, kc_vc, rows_new, cache_l, page_table, win_state, win_new):
    b, nt = q_raw.shape[0], q_raw.shape[1]
    npg = page_table.shape[1]
    n_past = npg * PAGE_SIZE
    n_cmp_pad = kc_vc.shape[1]
    n_sel = -(-(n_past + nt) // SEL_BLOCK)
    n_sel_pad = -(-n_sel // 8) * 8
    eye = jnp.eye(NSA_KV_HEADS, dtype=BF16)

    def blockdiag(q):
        q5 = q.reshape(b, nt, NSA_KV_HEADS, NSA_GROUP, HEAD_DIM).transpose(0, 3, 2, 1, 4)
        q6 = q5[:, :, :, :, None, :] * eye[None, None, :, None, :, None]
        return q6.reshape(b, LANES, NSA_KV_WIDTH)

    gt = gate[:, :48].reshape(b, nt, NSA_KV_HEADS, NSA_GROUP, 3).transpose(0, 3, 2, 1, 4)
    gt = jnp.pad(gt.reshape(b, LANES, 3), ((0, 0), (0, 0), (0, LANES - 3)))
    flat = page_table.reshape(-1)

    def page_spec(j):
        return pl.BlockSpec((1, PAGE_SIZE, 512),
                            lambda bi, si, tbl: (tbl[bi * npg + si * SMP_PAGES + j], 0, 1))

    per_b = lambda a: pl.BlockSpec((1,) + a.shape[1:], lambda bi, si, tbl: (bi,) + (0,) * (a.ndim - 1))
    qr, qo = blockdiag(q_raw), blockdiag(q_rot)
    kc = kc_vc[:, :, 0:256]
    vc = kc_vc[:, :, 256:512]
    o = pl.pallas_call(
        functools.partial(_nsa_sample_kernel, n_cmp_pad=n_cmp_pad, n_sel=n_sel, n_sel_pad=n_sel_pad,
                          n_past=n_past),
        grid_spec=pltpu.PrefetchScalarGridSpec(
            num_scalar_prefetch=1, grid=(b, npg // SMP_PAGES),
            in_specs=[page_spec(j) for j in range(SMP_PAGES)] + [
                per_b(qr), per_b(qo), per_b(kc), per_b(vc), per_b(rows_new), per_b(win_state),
                per_b(win_new), per_b(gt)],
            out_specs=pl.BlockSpec((1, LANES, NSA_KV_WIDTH), lambda bi, si, tbl: (bi, 0, 0)),
            scratch_shapes=[pltpu.VMEM((4 * n_sel_pad + 16, LANES), F32), pltpu.VMEM((n_sel_pad, LANES), F32),
                            pltpu.VMEM((1, LANES), F32), pltpu.VMEM((1, LANES), F32),
                            pltpu.VMEM((LANES, NSA_KV_WIDTH), F32), pltpu.VMEM((LANES, NSA_KV_WIDTH), F32)]),
        out_shape=jax.ShapeDtypeStruct((b, LANES, NSA_KV_WIDTH), F32),
        compiler_params=_cparams("parallel", "arbitrary"),
        name="nsa_sample_attn",
    )(flat, *([cache_l] * SMP_PAGES), qr, qo, kc, vc, rows_new, win_state, win_new, gt)
    o6 = o.reshape(b, NSA_GROUP, NSA_KV_HEADS, nt, NSA_KV_HEADS, HEAD_DIM)
    o5 = jnp.stack([o6[:, :, gg, :, gg, :] for gg in range(NSA_KV_HEADS)], axis=2)
    return o5.transpose(0, 3, 2, 1, 4).reshape(b, nt, 1024)


def _nsa_prompt_layer(x, g, w_in, cmp, w_out):
    t = x.shape[1]
    x2 = x.reshape(t, D_MODEL)
    q_raw, q_rot, rows, rows_b, win, win_b, sz, gate = _nsa_project(
        x2, g, w_in, jnp.arange(t, dtype=jnp.int32), 256)
    npg = t // PAGE_SIZE
    table = jnp.arange(npg, dtype=jnp.int32).reshape(1, npg)
    kc_vc = _compress(rows.reshape(npg, PAGE_SIZE, 1024), table, *cmp)[0]
    o = _nsa_prompt_attend(q_raw, q_rot, gate, kc_vc[:, 0:256], kc_vc[:, 256:512], rows_b, win_b)
    y = _nsa_out(o, sz, x2, w_out, 512)
    keep = min(NSA_WINDOW, t)
    return (y.reshape(1, t, D_MODEL), rows.reshape(1, t, 4, NSA_KV_HEADS, HEAD_DIM),
            win[t - keep:].reshape(1, keep, 2, NSA_KV_HEADS, HEAD_DIM))


def _nsa_sample_layer(x, cache_l, page_table, win_buf, g, w_in, cmp, w_out):
    b, nt, _ = x.shape
    m = b * nt
    n_past = page_table.shape[1] * PAGE_SIZE
    x2 = x.reshape(m, D_MODEL)
    pos = jnp.tile(n_past + jnp.arange(nt, dtype=jnp.int32), b)
    q_raw, q_rot, rows, _, win, _, sz, gate = _nsa_project(x2, g, w_in, pos, m)
    cache3 = cache_l.reshape(cache_l.shape[0], PAGE_SIZE, 1024)
    kc_vc = _compress(cache3, page_table, *cmp)
    n_buf = win_buf.shape[1]
    win_state = win_buf.reshape(b, n_buf, 512)
    win_new = win.reshape(b, nt, 512)
    o = _nsa_sample_attend(q_raw.reshape(b, nt, 1024), q_rot.reshape(b, nt, 1024), gate, kc_vc,
                           rows.reshape(b, nt, 1024), cache3, page_table, win_state, win_new)
    y = _nsa_out(o.reshape(m, 1024), sz, x2, w_out, m)
    wseq = jnp.concatenate([win_state, win_new], axis=1)
    keep = min(NSA_WINDOW, n_buf + nt)
    return (y.reshape(b, nt, D_MODEL), rows.reshape(b, nt, 4, NSA_KV_HEADS, HEAD_DIM),
            wseq[:, n_buf + nt - keep:].reshape(b, keep, 2, NSA_KV_HEADS, HEAD_DIM))


def _dil_prompt_layer(x, g, w_in, w_out, gf):
    t = x.shape[1]
    x2 = x.reshape(t, D_MODEL)
    proj = _dil_project(x2, g, w_in, jnp.arange(t, dtype=jnp.int32), 256)
    outs, lses, new = [], [], []
    for (q, kv, kvb, *_), (w, d) in zip(proj, DIL_PAIRS):
        o, lse = _dil_prompt_attend(q, kvb, d)
        outs.append(o)
        lses.append(lse)
        keep = min(w, t)
        new.append(kv[t - keep:].reshape(1, keep, 2, DIL_HEADS, HEAD_DIM))
    y = _dil_out(jnp.stack(outs), jnp.stack(lses), proj[-1][3], x2, w_out, gf, 256)
    return y.reshape(1, t, D_MODEL), new


def _dil_sample_layer(x, bufs, n_past, g, w_in, w_out, gf):
    b, nt, _ = x.shape
    m = b * nt
    x2 = x.reshape(m, D_MODEL)
    pos = jnp.tile(n_past + jnp.arange(nt, dtype=jnp.int32), b)
    proj = _dil_project(x2, g, w_in, pos, m)
    eye = jnp.eye(DIL_HEADS, dtype=BF16)
    outs, lses, new = [], [], []
    for (q, kv, _, *_), buf, (w, d) in zip(proj, bufs, DIL_PAIRS):
        n_buf = buf.shape[1]
        state = buf.reshape(b, n_buf, 2048)
        new_kv = kv.reshape(b, nt, 2048)
        q4 = q.reshape(b, nt, DIL_HEADS, HEAD_DIM).transpose(0, 2, 3, 1)
        qbd = q4[:, :, :, None, :] * eye[None, :, None, :, None]
        o, lse = _dil_sample_attend(state, new_kv, qbd.reshape(b, 1024, DIL_HEADS * nt), n_buf, d)
        outs.append(o.reshape(m, 1024))
        lses.append(lse.reshape(m, 1024))
        seq = jnp.concatenate([state, new_kv], axis=1)
        keep = min(w, n_buf + nt)
        new.append(seq[:, n_buf + nt - keep:].reshape(b, keep, 2, DIL_HEADS, HEAD_DIM))
    y = _dil_out(jnp.stack(outs), jnp.stack(lses), proj[-1][3], x2, w_out, gf, m)
    return y.reshape(b, nt, D_MODEL), new


def kernel(x_prompt, x_sample, cache_nsa_kv, state_nsa_win, state_dil_g0, state_dil_g1, state_dil_g2,
           page_table, nsa_norm, nsa_w_in, nsa_cmp_w1, nsa_cmp_b1, nsa_cmp_w2, nsa_cmp_pos, nsa_w_out,
           dil_norm, dil_w_in, dil_w_out, final_norm):
    n_past = page_table.shape[1] * PAGE_SIZE
    cmp = (nsa_cmp_w1[0], nsa_cmp_b1[0], nsa_cmp_w2[0], nsa_cmp_pos[0])
    y_p, rows_p, win_p = _nsa_prompt_layer(x_prompt, nsa_norm[0], nsa_w_in[0], cmp, nsa_w_out[0])
    y_s, rows_s, win_s = _nsa_sample_layer(x_sample, cache_nsa_kv[0], page_table, state_nsa_win[0],
                                           nsa_norm[0], nsa_w_in[0], cmp, nsa_w_out[0])
    y_p, new_p = _dil_prompt_layer(y_p, dil_norm[0], dil_w_in[0], dil_w_out[0], final_norm)
    y_s, new_s = _dil_sample_layer(y_s, (state_dil_g0[0], state_dil_g1[0], state_dil_g2[0]), n_past,
                                   dil_norm[0], dil_w_in[0], dil_w_out[0], final_norm)
    add = lambda a: a[None]
    return (y_p, y_s, add(rows_p), add(rows_s), add(win_p), add(win_s),
            add(new_p[0]), add(new_s[0]), add(new_p[1]), add(new_s[1]), add(new_p[2]), add(new_s[2]))
```

```python
import functools

import jax
import jax.numpy as jnp
from jax import lax
from jax.experimental import pallas as pl
from jax.experimental.pallas import tpu as pltpu

F32 = jnp.float32
BF16 = jnp.bfloat16

D_MODEL = 1024
HEAD_DIM = 64
ROT_DIM = HEAD_DIM // 4
ROPE_THETA = 500000.0
NORM_EPS = 1e-6
NEG_INF = -1e30
Q_SCALE = HEAD_DIM ** -0.5
PAGE_SIZE = 128

NSA_KV_HEADS = 4
NSA_GROUP = 4
NSA_KV_WIDTH = NSA_KV_HEADS * HEAD_DIM
CMP_BLOCK = 32
CMP_STRIDE = 16
SEL_BLOCK = 64
SEL_TOP_N = 16
NSA_WINDOW = 512
FORCE_BONUS = 1e4
DIL_PAIRS = ((128, 1), (512, 4), (2048, 16))
DIL_HEADS = 16

LANES = 128
Q_TILE = 128
SEL_HALF = 256
SEL_HALVES = 4
PROJ_TILE = 256
VMEM_LIMIT = 56 * 1024 * 1024

_NT = (((1,), (1,)), ((), ()))


def _cparams(*sem):
    return pltpu.CompilerParams(dimension_semantics=sem, vmem_limit_bytes=VMEM_LIMIT)


def _rope_tables(pos):
    half = ROT_DIM // 2
    inv = ROPE_THETA ** (-jnp.arange(half, dtype=F32) / half)
    ang = pos.astype(F32)[:, None] * inv[None, :]
    cos, sin = jnp.cos(ang), jnp.sin(ang)
    m = pos.shape[0]
    one, zero, z8 = jnp.ones((m, 48), F32), jnp.zeros((m, 48), F32), jnp.zeros((m, 8), F32)
    c = jnp.concatenate([cos, cos, one], axis=1)
    s1 = jnp.concatenate([-sin, z8, zero], axis=1)
    s2 = jnp.concatenate([z8, sin, zero], axis=1)
    return jnp.concatenate([c, c, s1, s1, s2, s2], axis=1), cos.T, sin.T


def _rope(y, tab_ref):
    c, s1, s2 = tab_ref[:, 0:128], tab_ref[:, 128:256], tab_ref[:, 256:384]
    out = []
    for j in range(y.shape[1] // LANES):
        yc = y[:, j * LANES:(j + 1) * LANES]
        out.append(yc * c + pltpu.roll(yc, LANES - 8, 1) * s1 + pltpu.roll(yc, 8, 1) * s2)
    return out[0] if len(out) == 1 else jnp.concatenate(out, axis=1)


def _rope_t(y, cos, sin):
    out = []
    for h in range(y.shape[0] // HEAD_DIM):
        b = h * HEAD_DIM
        x1, x2 = y[b:b + 8], y[b + 8:b + 16]
        out += [x1 * cos - x2 * sin, x2 * cos + x1 * sin, y[b + 16:b + HEAD_DIM]]
    return jnp.concatenate(out, axis=0)


def _normed_bf16(x_ref, g_ref):
    x = x_ref[...]
    ms = jnp.mean(x * x, axis=-1, keepdims=True)
    return ((x * lax.rsqrt(ms + NORM_EPS)) * g_ref[...]).astype(BF16)


def _silu(z):
    return z * jax.nn.sigmoid(z)


def _softmax_cols(s, mask):
    s = jnp.where(mask, s, NEG_INF)
    m = jnp.max(s, axis=0, keepdims=True)
    e = jnp.where(mask, jnp.exp(s - m), 0.0)
    den = jnp.sum(e, axis=0, keepdims=True)
    return e * (1.0 / jnp.maximum(den, 1e-30))


def _softmax_rows2(s_a, m_a, s_b, m_b):
    s_a = jnp.where(m_a, s_a, NEG_INF)
    s_b = jnp.where(m_b, s_b, NEG_INF)
    m = jnp.maximum(jnp.max(s_a, axis=1, keepdims=True), jnp.max(s_b, axis=1, keepdims=True))
    e_a = jnp.where(m_a, jnp.exp(s_a - m), 0.0)
    e_b = jnp.where(m_b, jnp.exp(s_b - m), 0.0)
    den = jnp.sum(e_a, axis=1, keepdims=True) + jnp.sum(e_b, axis=1, keepdims=True)
    inv = 1.0 / jnp.maximum(den, 1e-30)
    return e_a * inv, e_b * inv, m + jnp.log(den)


def _nsa_proj_kernel(x_ref, g_ref, wnn_ref, wt_ref, tab_ref, cos_ref, sin_ref,
                     qraw_ref, qrot_ref, rowst_ref, cmp_ref, ks_ref, vst_ref, wint_ref, kw_ref, vwt_ref,
                     sz_ref, gatet_ref):
    tm = x_ref.shape[0]
    hb = _normed_bf16(x_ref, g_ref)
    nn = lambda a, b: jnp.dot(hb, wnn_ref[:, a:b], preferred_element_type=F32)
    nt = lambda a, b: lax.dot_general(wt_ref[a:b, :], hb, _NT, preferred_element_type=F32)
    cos, sin = cos_ref[...], sin_ref[...]
    q = nn(0, 1024) * Q_SCALE
    qraw_ref[...] = q.astype(BF16)
    qrot_ref[...] = _rope(q, tab_ref).astype(BF16)
    kcs = nn(1024, 1792)
    cmp_ref[...] = kcs[:, :512]
    ks = _rope(kcs[:, 512:768], tab_ref)
    kw = _rope(nn(1792, 2048), tab_ref)
    row = pl.program_id(0) * tm + lax.broadcasted_iota(jnp.int32, (tm, LANES), 0)
    lane = lax.broadcasted_iota(jnp.int32, (tm, LANES), 1)
    mark = jnp.where(lane - HEAD_DIM == (row // SEL_BLOCK) % HEAD_DIM, 1.0, 0.0)
    for g in range(NSA_KV_HEADS):
        pair = ks[:, (g // 2) * LANES:(g // 2 + 1) * LANES]
        if g % 2:
            pair = pltpu.roll(pair, HEAD_DIM, 1)
        ks_ref[g] = jnp.where(lane < HEAD_DIM, pair, mark).astype(BF16)
        kw_ref[g] = kw[:, g * HEAD_DIM:(g + 1) * HEAD_DIM].astype(BF16)
    sz_ref[...] = _silu(nn(2048, 3072))
    rt = nt(0, 1024)
    rowst_ref[...] = jnp.concatenate([rt[:512], _rope_t(rt[512:768], cos, sin), rt[768:]], axis=0)
    vs = rt[768:1024].astype(BF16)
    wt = nt(1024, 1536)
    wint_ref[...] = jnp.concatenate([_rope_t(wt[:256], cos, sin), wt[256:]], axis=0)
    vw = wt[256:512].astype(BF16)
    for g in range(NSA_KV_HEADS):
        hs = slice(g * HEAD_DIM, (g + 1) * HEAD_DIM)
        for c in range(tm // SEL_HALF):
            vst_ref[g, c] = vs[hs, c * SEL_HALF:(c + 1) * SEL_HALF]
        for c in range(tm // Q_TILE):
            vwt_ref[g, c] = vw[hs, c * Q_TILE:(c + 1) * Q_TILE]
    gatet_ref[...] = jax.nn.sigmoid(nt(1536, 1664))


def _nsa_project(x2, g, w_in, pos):
    m = x2.shape[0]
    tm = PROJ_TILE
    w_nn = jnp.concatenate([w_in[:, 0:1024], w_in[:, 1024:1792], w_in[:, 2048:2304], w_in[:, 2608:3632]],
                           axis=1).astype(BF16)
    gl = w_in[:, 2560:2608].T.reshape(NSA_KV_HEADS, NSA_GROUP, 3, D_MODEL).transpose(0, 2, 1, 3)
    gl = jnp.pad(gl, ((0, 0), (0, 1), (0, 8 - NSA_GROUP), (0, 0))).reshape(LANES, D_MODEL)
    w_t = jnp.concatenate([w_in[:, 1024:2048].T, w_in[:, 2048:2560].T, gl], axis=0).astype(BF16)
    tab, cos_t, sin_t = _rope_tables(pos)
    g2 = g.reshape(1, D_MODEL)
    row = lambda n: pl.BlockSpec((tm, n), lambda i: (i, 0))
    col = lambda n: pl.BlockSpec((n, tm), lambda i: (0, i))
    full = lambda a: pl.BlockSpec(a.shape, lambda i: (0,) * a.ndim)
    out = [
        (row(1024), (m, 1024), BF16), (row(1024), (m, 1024), BF16),
        (col(1024), (1024, m), F32), (row(512), (m, 512), F32),
        (pl.BlockSpec((NSA_KV_HEADS, tm, LANES), lambda i: (0, i, 0)), (NSA_KV_HEADS, m, LANES), BF16),
        (pl.BlockSpec((NSA_KV_HEADS, tm // SEL_HALF, HEAD_DIM, SEL_HALF), lambda i: (0, i, 0, 0)),
         (NSA_KV_HEADS, m // SEL_HALF, HEAD_DIM, SEL_HALF), BF16),
        (col(512), (512, m), F32),
        (pl.BlockSpec((NSA_KV_HEADS, tm, HEAD_DIM), lambda i: (0, i, 0)), (NSA_KV_HEADS, m, HEAD_DIM), BF16),
        (pl.BlockSpec((NSA_KV_HEADS, tm // Q_TILE, HEAD_DIM, Q_TILE), lambda i: (0, i, 0, 0)),
         (NSA_KV_HEADS, m // Q_TILE, HEAD_DIM, Q_TILE), BF16),
        (row(1024), (m, 1024), F32), (col(LANES), (LANES, m), F32),
    ]
    return pl.pallas_call(
        _nsa_proj_kernel,
        grid=(m // tm,),
        in_specs=[row(D_MODEL), full(g2), full(w_nn), full(w_t), row(384), col(8), col(8)],
        out_specs=[o[0] for o in out],
        out_shape=[jax.ShapeDtypeStruct(o[1], o[2]) for o in out],
        compiler_params=_cparams("parallel"),
        name="nsa_proj",
    )(x2, g2, w_nn, w_t, tab, cos_t, sin_t)


def _dil_proj_kernel(x_ref, g_ref, wnn_ref, wt_ref, tab_ref, cos_ref, sin_ref, *out_refs,
                     with_kvb, with_z, tail_steps, tail_lanes):
    out_refs = list(out_refs)
    q_ref = out_refs.pop(0)
    kvb_ref = out_refs.pop(0) if with_kvb else None
    kvt_ref = out_refs.pop(0)
    sz_ref = out_refs.pop(0) if with_z else None
    tm = x_ref.shape[0]
    hb = _normed_bf16(x_ref, g_ref)
    nn = lambda a, b: jnp.dot(hb, wnn_ref[:, a:b], preferred_element_type=F32)
    q_ref[...] = _rope(nn(0, 1024) * Q_SCALE, tab_ref).astype(BF16)
    if with_kvb:
        kvb_ref[:, :1024] = _rope(nn(1024, 2048), tab_ref).astype(BF16)
        kvb_ref[:, 1024:] = nn(2048, 3072).astype(BF16)
    if with_z:
        sz_ref[...] = _silu(nn(3072, 4096))

    @pl.when(pl.program_id(0) >= pl.num_programs(0) - tail_steps)
    def _():
        kt = lax.dot_general(wt_ref[0:1024, :], hb, _NT, preferred_element_type=F32)
        kt = _rope_t(kt, cos_ref[...], sin_ref[...])
        vt = lax.dot_general(wt_ref[1024:2048, :], hb, _NT, preferred_element_type=F32)
        kvt_ref[0:1024, :] = kt[:, tm - tail_lanes:]
        kvt_ref[1024:2048, :] = vt[:, tm - tail_lanes:]


def _dil_project(x2, g, w_in, pos, keeps, with_kvb):
    m = x2.shape[0]
    tm = PROJ_TILE
    n = m // tm
    tab, cos_t, sin_t = _rope_tables(pos)
    g2 = g.reshape(1, D_MODEL)
    row = lambda c: pl.BlockSpec((tm, c), lambda i: (i, 0))
    col = lambda c: pl.BlockSpec((c, tm), lambda i: (0, i))
    full = lambda a: pl.BlockSpec(a.shape, lambda i: (0,) * a.ndim)
    outs = []
    for gi, keep in enumerate(keeps):
        last = gi == len(keeps) - 1
        wg = w_in[:, 3072 * gi:3072 * (gi + 1)]
        cols = [wg[:, :1024]] + ([wg[:, 1024:]] if with_kvb else []) + ([w_in[:, 9216:]] if last else [])
        if not with_kvb and last:
            cols.insert(1, jnp.zeros((D_MODEL, 2048), F32))
        w_nn = jnp.concatenate(cols, axis=1).astype(BF16)
        w_t = wg[:, 1024:].T.astype(BF16)
        tail_steps = max(keep // tm, 1)
        tail_lanes = min(keep, tm)
        spec = [(row(1024), (m, 1024), BF16)]
        if with_kvb:
            spec.append((row(2048), (m, 2048), BF16))
        spec.append((pl.BlockSpec((2048, tail_lanes), lambda i: (0, jnp.maximum(i - (n - tail_steps), 0))),
                     (2048, keep), F32))
        if last:
            spec.append((row(1024), (m, 1024), F32))
        outs.append(pl.pallas_call(
            functools.partial(_dil_proj_kernel, with_kvb=with_kvb, with_z=last, tail_steps=tail_steps,
                              tail_lanes=tail_lanes),
            grid=(n,),
            in_specs=[row(D_MODEL), full(g2), full(w_nn), full(w_t), row(384), col(8), col(8)],
            out_specs=[s[0] for s in spec],
            out_shape=[jax.ShapeDtypeStruct(s[1], s[2]) for s in spec],
            compiler_params=_cparams("arbitrary"),
            name=f"dil_proj_g{gi}",
        )(x2, g2, w_nn, w_t, tab, cos_t, sin_t))
    return outs


def _nsa_out_kernel(o_ref, sz_ref, x_ref, w_ref, y_ref):
    a = (o_ref[...] * sz_ref[...]).astype(BF16)
    y_ref[...] = x_ref[...] + jnp.dot(a, w_ref[...], preferred_element_type=F32)


def _dil_out_kernel(o0_ref, o1_ref, o2_ref, l0_ref, l1_ref, l2_ref, sz_ref, x_ref, w_ref, gf_ref, y_ref):
    l0, l1, l2 = l0_ref[...], l1_ref[...], l2_ref[...]
    m = jnp.maximum(jnp.maximum(l0, l1), l2)
    e0, e1, e2 = jnp.exp(l0 - m), jnp.exp(l1 - m), jnp.exp(l2 - m)
    inv = 1.0 / (e0 + e1 + e2)
    o = (e0 * inv) * o0_ref[...] + (e1 * inv) * o1_ref[...] + (e2 * inv) * o2_ref[...]
    a = (o * sz_ref[...]).astype(BF16)
    y = x_ref[...] + jnp.dot(a, w_ref[...], preferred_element_type=F32)
    ms = jnp.mean(y * y, axis=-1, keepdims=True)
    y_ref[...] = (y * lax.rsqrt(ms + NORM_EPS)) * gf_ref[...]


def _nsa_out(o, sz, x2, w_out, tm):
    m = x2.shape[0]
    row = pl.BlockSpec((tm, D_MODEL), lambda i: (i, 0))
    w = w_out.astype(BF16)
    return pl.pallas_call(
        _nsa_out_kernel, grid=(m // tm,),
        in_specs=[row, row, row, pl.BlockSpec(w.shape, lambda i: (0, 0))],
        out_specs=row, out_shape=jax.ShapeDtypeStruct((m, D_MODEL), F32),
        compiler_params=_cparams("parallel"), name="nsa_out",
    )(o, sz, x2, w)


def _dil_out(outs, lses, sz, x2, w_out, gf, tm):
    m = x2.shape[0]
    row = pl.BlockSpec((tm, D_MODEL), lambda i: (i, 0))
    w = w_out.astype(BF16)
    gf2 = gf.reshape(1, D_MODEL)
    return pl.pallas_call(
        _dil_out_kernel, grid=(m // tm,),
        in_specs=[row] * 8 + [pl.BlockSpec(w.shape, lambda i: (0, 0)), pl.BlockSpec(gf2.shape, lambda i: (0, 0))],
        out_specs=row, out_shape=jax.ShapeDtypeStruct((m, D_MODEL), F32),
        compiler_params=_cparams("parallel"), name="dil_out",
    )(*outs, *lses, sz, x2, w, gf2)


def _dil_prompt_kernel(q_ref, kp_ref, kc_ref, vp_ref, vc_ref, o_ref, lse_ref):
    ib = pl.program_id(1)
    tq = lax.broadcasted_iota(jnp.int32, (Q_TILE, 2 * Q_TILE), 0)
    kk = lax.broadcasted_iota(jnp.int32, (Q_TILE, 2 * Q_TILE), 1)
    diff = kk - tq
    lo_key = jnp.where(ib > 0, 0, Q_TILE)
    mask = jnp.logical_and(jnp.logical_and(diff >= 0, diff <= Q_TILE), kk >= lo_key)
    lo = lax.broadcasted_iota(jnp.int32, (Q_TILE, LANES), 1) < HEAD_DIM
    for pr in range(DIL_HEADS // 2):
        cs = slice(pr * LANES, (pr + 1) * LANES)
        q2 = q_ref[:, cs]
        k2 = jnp.concatenate([kp_ref[:, cs], kc_ref[:, cs]], axis=0)
        v2 = jnp.concatenate([vp_ref[:, cs], vc_ref[:, cs]], axis=0)
        outs, lses = [], []
        for half in range(2):
            keep = lo if half == 0 else jnp.logical_not(lo)
            qm = jnp.where(keep, q2, jnp.zeros_like(q2))
            s = lax.dot_general(qm, k2, _NT, preferred_element_type=F32)
            s = jnp.where(mask, s, NEG_INF)
            m = jnp.max(s, axis=-1, keepdims=True)
            e = jnp.where(mask, jnp.exp(s - m), 0.0)
            den = jnp.sum(e, axis=-1, keepdims=True)
            p = e * (1.0 / jnp.maximum(den, 1e-30))
            outs.append(jnp.dot(p.astype(BF16), v2, preferred_element_type=F32))
            lses.append(jnp.broadcast_to(m + jnp.log(den), (Q_TILE, LANES)))
        o_ref[:, cs] = jnp.where(lo, outs[0], outs[1])
        lse_ref[:, cs] = jnp.where(lo, lses[0], lses[1])


def _dil_prompt_attend(q, kvb, d):
    t = q.shape[0]
    n = t // d
    qv = q.reshape(n, d * 1024)
    kvv = kvb.reshape(n, d * 2048)
    blk = lambda f: pl.BlockSpec((Q_TILE, 1024), f)
    prev = lambda i: jnp.maximum(i - 1, 0)
    out = pl.pallas_call(
        _dil_prompt_kernel,
        grid=(d, n // Q_TILE),
        in_specs=[blk(lambda r, i: (i, r)),
                  blk(lambda r, i: (prev(i), 2 * r)), blk(lambda r, i: (i, 2 * r)),
                  blk(lambda r, i: (prev(i), 2 * r + 1)), blk(lambda r, i: (i, 2 * r + 1))],
        out_specs=[blk(lambda r, i: (i, r)), blk(lambda r, i: (i, r))],
        out_shape=[jax.ShapeDtypeStruct((n, d * 1024), F32)] * 2,
        compiler_params=_cparams("parallel", "arbitrary"),
        name=f"dil_prompt_d{d}",
    )(qv, kvv, kvv, kvv, kvv)
    return out[0].reshape(t, 1024), out[1].reshape(t, 1024)


DS_HEADS = 4


def _dil_sample_kernel(st_ref, newp_ref, qbd_ref, ost_ref, o_ref, lse_ref, *, w, d):
    nt = o_ref.shape[1]
    rows = DS_HEADS * HEAD_DIM
    nq = DS_HEADS * nt
    new0 = LANES - nt
    kt, vt = st_ref[0, 0].reshape(rows, w), st_ref[0, 1].reshape(rows, w)
    kn, vn = newp_ref[0, 0].reshape(rows, LANES), newp_ref[0, 1].reshape(rows, LANES)
    qbd = qbd_ref[0, 0]
    s_st = jnp.dot(qbd, kt.astype(BF16), preferred_element_type=F32)
    s_nw = jnp.dot(qbd, kn.astype(BF16), preferred_element_type=F32)
    t_row = lax.broadcasted_iota(jnp.int32, (nq, 1), 0) % nt
    diff_st = (w + t_row) - lax.broadcasted_iota(jnp.int32, (1, w), 1)
    m_st = jnp.logical_and((diff_st & (d - 1)) == 0, diff_st <= w)
    i_nw = lax.broadcasted_iota(jnp.int32, (1, LANES), 1) - new0
    diff_nw = t_row - i_nw
    m_nw = jnp.logical_and(jnp.logical_and(i_nw >= 0, diff_nw >= 0), (diff_nw & (d - 1)) == 0)
    p_st, p_nw, lse = _softmax_rows2(s_st, m_st, s_nw, m_nw)
    acc = (lax.dot_general(p_st.astype(BF16), vt.astype(BF16), _NT, preferred_element_type=F32)
           + lax.dot_general(p_nw.astype(BF16), vn.astype(BF16), _NT, preferred_element_type=F32))
    diag = (lax.broadcasted_iota(jnp.int32, (nq, rows), 0) // nt
            == lax.broadcasted_iota(jnp.int32, (nq, rows), 1) // HEAD_DIM)

    def fold(a):
        a = jnp.where(diag, a, 0.0)
        out = a[0:nt]
        for h in range(1, DS_HEADS):
            out = out + a[h * nt:(h + 1) * nt]
        return out

    o_ref[0] = fold(acc)
    lse_ref[0] = fold(jnp.broadcast_to(lse, (nq, rows)))
    is_new = lax.broadcasted_iota(jnp.int32, (rows, LANES), 1) >= new0
    for si, (old, new) in enumerate(((kt, kn), (vt, vn))):
        rolled = pltpu.roll(old, w - nt, 1)
        tail = jnp.where(is_new, new, rolled[:, w - LANES:])
        full = tail if w == LANES else jnp.concatenate([rolled[:, :w - LANES], tail], axis=1)
        ost_ref[0, si] = full.reshape(DS_HEADS, HEAD_DIM, w)


def _dil_sample_attend(state_t, newp, qbd, d):
    b, w = state_t.shape[0], state_t.shape[-1]
    nt = qbd.shape[2] // DS_HEADS
    rows = DS_HEADS * HEAD_DIM
    st_spec = lambda n: pl.BlockSpec((1, 2, DS_HEADS, HEAD_DIM, n), lambda i, c: (i, 0, c, 0, 0))
    o_spec = pl.BlockSpec((1, nt, rows), lambda i, c: (i, 0, c))
    return pl.pallas_call(
        functools.partial(_dil_sample_kernel, w=w, d=d),
        grid=(b, DIL_HEADS // DS_HEADS),
        in_specs=[st_spec(w), st_spec(LANES), pl.BlockSpec((1, 1, DS_HEADS * nt, rows), lambda i, c: (i, c, 0, 0))],
        out_specs=[st_spec(w), o_spec, o_spec],
        out_shape=[jax.ShapeDtypeStruct(state_t.shape, F32), jax.ShapeDtypeStruct((b, nt, 1024), F32),
                   jax.ShapeDtypeStruct((b, nt, 1024), F32)],
        compiler_params=_cparams("parallel", "parallel"),
        name=f"dil_sample_d{d}",
    )(state_t, newp, qbd)


CHUNKS_PER_PAGE = PAGE_SIZE // CMP_STRIDE


def _compress_kernel(tbl_ref, *refs, native, n_pages):
    del tbl_ref
    pages = refs[:n_pages + 1]
    perm_ref, w1_ref, pre_ref, w2_ref, out_ref, xp_ref, shift_ref = refs[n_pages + 1:]
    nch = n_pages * CHUNKS_PER_PAGE
    perm = perm_ref[...]
    for j, p in enumerate(pages):
        if native:
            for st in range(2):
                xt = p[0, st].reshape(NSA_KV_WIDTH, PAGE_SIZE).astype(BF16)
                xp_ref[j, :, st * 256:(st + 1) * 256] = lax.dot_general(perm, xt, _NT, preferred_element_type=F32)
        else:
            xp_ref[j] = jnp.dot(perm, p[0].astype(BF16), preferred_element_type=F32)
    pad = jnp.zeros((CHUNKS_PER_PAGE, 512), F32)
    acc = [[None, None], [None, None]]
    for s in range(CMP_STRIDE):
        rs = slice(s * CHUNKS_PER_PAGE, (s + 1) * CHUNKS_PER_PAGE)
        x = jnp.concatenate([xp_ref[j, rs, :] for j in range(n_pages + 1)] + [pad], axis=0).astype(BF16)
        for st in range(2):
            xs = x[:, st * 256:(st + 1) * 256]
            for r in range(2):
                part = jnp.dot(xs, w1_ref[st, r, s], preferred_element_type=F32)
                acc[st][r] = part if acc[st][r] is None else acc[st][r] + part
    outs = []
    for st in range(2):
        shift_ref[...] = acc[st][1]
        pre = acc[st][0][0:nch] + shift_ref[pl.ds(1, nch), :] + pre_ref[:, st * 256:(st + 1) * 256]
        outs.append(jnp.dot(_silu(pre).astype(BF16), w2_ref[st], preferred_element_type=F32))
    out_ref[0] = jnp.concatenate(outs, axis=1)


def _blockdiag4(a):
    eye = jnp.eye(NSA_KV_HEADS, dtype=a.dtype)
    out = a[..., None, :, None, :] * eye[:, None, :, None]
    return out.reshape(a.shape[:-2] + (NSA_KV_HEADS * a.shape[-2], NSA_KV_HEADS * a.shape[-1]))


def _cmp_bias_kernel(pos_ref, w1_ref, b1_ref, out_ref):
    for st in range(2):
        pos = jnp.broadcast_to(pos_ref[st], (8, CMP_BLOCK * HEAD_DIM)).astype(BF16)
        out_ref[st] = b1_ref[st] + jnp.dot(pos, w1_ref[st].astype(BF16), preferred_element_type=F32)


def _compress(src, table, cmp_w1, cmp_b1, cmp_w2, cmp_pos, native):
    b, npg = table.shape
    n_pages = 32 if npg % 32 == 0 else 16
    steps = npg // n_pages
    r = CMP_BLOCK // CMP_STRIDE
    w1r = cmp_w1.reshape(2, r, CMP_STRIDE, HEAD_DIM, HEAD_DIM)
    w1bd = _blockdiag4(w1r).astype(BF16)
    w2bd = _blockdiag4(cmp_w2).astype(BF16)
    pre = pl.pallas_call(
        _cmp_bias_kernel,
        out_shape=jax.ShapeDtypeStruct((2, 8, HEAD_DIM), F32), name="cmp_bias",
    )(cmp_pos.reshape(2, 1, CMP_BLOCK * HEAD_DIM), cmp_w1, cmp_b1.reshape(2, 1, HEAD_DIM))
    pre = jnp.tile(pre[:, 0, :], (1, NSA_KV_HEADS)).reshape(1, 512)
    flat = table.reshape(-1)
    blk = (1, 2, NSA_KV_HEADS, HEAD_DIM, PAGE_SIZE) if native else (1, PAGE_SIZE, 512)
    out_row = jnp.arange(PAGE_SIZE)
    src_row = (out_row % CHUNKS_PER_PAGE) * CMP_STRIDE + out_row // CHUNKS_PER_PAGE
    perm = (src_row[:, None] == jnp.arange(PAGE_SIZE)[None, :]).astype(BF16)

    def page_spec(j):
        return pl.BlockSpec(
            blk, lambda bi, si, tbl: (tbl[bi * npg + jnp.minimum(si * n_pages + j, npg - 1)],) + (0,) * (len(blk) - 1))

    const = lambda a: pl.BlockSpec(a.shape, lambda bi, si, tbl: (0,) * a.ndim)
    nch = n_pages * CHUNKS_PER_PAGE
    return pl.pallas_call(
        functools.partial(_compress_kernel, native=native, n_pages=n_pages),
        grid_spec=pltpu.PrefetchScalarGridSpec(
            num_scalar_prefetch=1, grid=(b, steps),
            in_specs=[page_spec(j) for j in range(n_pages + 1)] + [const(perm), const(w1bd), const(pre), const(w2bd)],
            out_specs=pl.BlockSpec((1, nch, 512), lambda bi, si, tbl: (bi, si, 0)),
            scratch_shapes=[pltpu.VMEM((n_pages + 1, PAGE_SIZE, 512), F32),
                            pltpu.VMEM((nch + 2 * CHUNKS_PER_PAGE, 256), F32)]),
        out_shape=jax.ShapeDtypeStruct((b, npg * CHUNKS_PER_PAGE, 512), F32),
        compiler_params=_cparams("parallel", "parallel"),
        name="nsa_compress",
    )(flat, *([src] * (n_pages + 1)), perm, w1bd, pre, w2bd)


def _select_bias(imp_ref, n_blk, n_rows, q_pos):
    cols = q_pos.shape[1]
    imp_s = imp_ref[pl.ds(7, n_rows, stride=4), :]
    for e in range(1, 5):
        imp_s = imp_s + imp_ref[pl.ds(7 + e, n_rows, stride=4), :]
    blk = lax.broadcasted_iota(jnp.int32, (n_rows, cols), 0)
    cur = q_pos // SEL_BLOCK
    valid = blk * SEL_BLOCK <= q_pos
    forced = jnp.logical_or(jnp.logical_or(blk == 0, blk == cur), blk == cur - 1)
    score = jnp.where(valid, imp_s, -1.0) + jnp.where(forced, FORCE_BONUS, 0.0)
    work = jnp.where(blk < n_blk, score, -jnp.inf)
    bias = jnp.full((n_rows, cols), NEG_INF, F32)
    for _ in range(min(SEL_TOP_N, n_blk)):
        mx = jnp.max(work, axis=0, keepdims=True)
        first = jnp.min(jnp.where(work == mx, blk, n_rows), axis=0, keepdims=True)
        pick = blk == first
        bias = jnp.where(pick, 0.0, bias)
        work = jnp.where(pick, -jnp.inf, work)
    return bias


def _nsa_prompt_kernel(qraw_ref, qrot_ref, kc_ref, vct_ref, ks_ref, vst_ref, kw_ref, vwt_ref,
                       gate_ref, o_ref, imp_ref, qaug_ref, sca_ref, scb_ref, *, n_cmp_pad, n_sel):
    qb = pl.program_id(1)
    s0 = qb * Q_TILE
    cols = NSA_GROUP * Q_TILE

    def stack(ref):
        x = ref[...].astype(F32)
        return jnp.concatenate([x[:, r * HEAD_DIM:(r + 1) * HEAD_DIM] for r in range(NSA_GROUP)],
                               axis=0).astype(BF16)

    q_raw, q_rot = stack(qraw_ref), stack(qrot_ref)
    t_col = lax.broadcasted_iota(jnp.int32, (1, cols), 1) % Q_TILE
    q_pos = s0 + t_col

    s_c = lax.dot_general(kc_ref[0], q_raw, _NT, preferred_element_type=F32)
    c_end = lax.broadcasted_iota(jnp.int32, (n_cmp_pad, 1), 0) * CMP_STRIDE + (CMP_BLOCK - 1)
    p_c = _softmax_cols(s_c, c_end <= q_pos)
    o_c = jnp.dot(vct_ref[0], p_c.astype(BF16), preferred_element_type=F32)

    imp = p_c[:, 0:Q_TILE]
    for r in range(1, NSA_GROUP):
        imp = imp + p_c[:, r * Q_TILE:(r + 1) * Q_TILE]
    imp_ref[0:8, :] = jnp.zeros((8, Q_TILE), F32)
    imp_ref[8:8 + n_cmp_pad, :] = imp
    imp_ref[8 + n_cmp_pad:16 + n_cmp_pad, :] = jnp.zeros((8, Q_TILE), F32)
    bias = _select_bias(imp_ref, n_sel, n_sel, q_pos[:, 0:Q_TILE])

    lane = lax.broadcasted_iota(jnp.int32, (Q_TILE, LANES), 1)
    n_tile = -(-n_sel // LANES)
    if n_tile * LANES > n_sel:
        bias = jnp.concatenate([bias, jnp.full((n_tile * LANES - n_sel, Q_TILE), NEG_INF, F32)], axis=0)
    x = qrot_ref[...].astype(F32)
    q_lo = []
    for r in range(NSA_GROUP):
        pair = x[:, (r // 2) * LANES:(r // 2 + 1) * LANES]
        q_lo.append(pltpu.roll(pair, HEAD_DIM, 1) if r % 2 else pair)
    for sp in range(qaug_ref.shape[0]):
        b_t = bias[(sp // 2) * LANES:(sp // 2 + 1) * LANES].T
        b_hi = b_t if sp % 2 else pltpu.roll(b_t, HEAD_DIM, 1)
        for r in range(NSA_GROUP):
            qaug_ref[sp, r * Q_TILE:(r + 1) * Q_TILE, :] = jnp.where(lane < HEAD_DIM, q_lo[r], b_hi).astype(BF16)

    span_groups = HEAD_DIM * SEL_BLOCK // (SEL_HALVES * SEL_HALF)

    def scores(kt, sc_ref):
        q_aug = qaug_ref[kt // span_groups]
        top = None
        for i in range(SEL_HALVES):
            k0 = pl.multiple_of((kt * SEL_HALVES + i) * SEL_HALF, SEL_HALF)
            s = lax.dot_general(ks_ref[0, pl.ds(k0, SEL_HALF), :], q_aug, _NT,
                                preferred_element_type=F32)
            sc_ref[i] = s
            top = s if top is None else jnp.maximum(top, s)
        return jnp.max(top, axis=0, keepdims=True)

    def update(kt, sc_ref, top, carry):
        m, l, acc = carry
        parts = [sc_ref[i] for i in range(SEL_HALVES)]
        if top is None:
            for i in range(SEL_HALVES):
                k_pos = (kt * SEL_HALVES + i) * SEL_HALF + lax.broadcasted_iota(jnp.int32, (SEL_HALF, 1), 0)
                parts[i] = jnp.where(k_pos <= q_pos, parts[i], NEG_INF)
                top = parts[i] if top is None else jnp.maximum(top, parts[i])
            top = jnp.max(top, axis=0, keepdims=True)
        m_new = jnp.maximum(m, top)
        alpha = jnp.exp(m - m_new)
        acc = alpha * acc
        p_sum = None
        for i, s in enumerate(parts):
            p = jnp.exp(s - m_new)
            p_sum = p if p_sum is None else p_sum + p
            acc = acc + jnp.dot(vst_ref[0, kt * SEL_HALVES + i], p.astype(BF16), preferred_element_type=F32)
        return m_new, alpha * l + jnp.sum(p_sum, axis=0, keepdims=True), acc

    n_main = s0 // (SEL_HALVES * SEL_HALF)

    def pair_step(j, carry):
        m, l, acc, top_a = carry
        top_b = scores(2 * j + 1, scb_ref)
        m, l, acc = update(2 * j, sca_ref, top_a, (m, l, acc))
        top_a = scores(2 * j + 2, sca_ref)
        m, l, acc = update(2 * j + 1, scb_ref, top_b, (m, l, acc))
        return m, l, acc, top_a

    def odd_tail(carry):
        m, l, acc, top_a = carry
        scores(n_main, scb_ref)
        m, l, acc = update(n_main - 1, sca_ref, top_a, (m, l, acc))
        return update(n_main, scb_ref, None, (m, l, acc))

    def even_tail(carry):
        m, l, acc, _ = carry
        return update(n_main, sca_ref, None, (m, l, acc))

    carry = (jnp.full((1, cols), NEG_INF, F32), jnp.zeros((1, cols), F32), jnp.zeros((HEAD_DIM, cols), F32),
             scores(0, sca_ref))
    carry = lax.fori_loop(0, n_main // 2, pair_step, carry)
    _, l, acc = lax.cond(n_main % 2 == 1, odd_tail, even_tail, carry)
    o_s = acc * (1.0 / l)

    n_w = NSA_WINDOW + Q_TILE
    w0 = jnp.maximum(s0 - NSA_WINDOW, 0)
    s_w = lax.dot_general(kw_ref[0, pl.ds(pl.multiple_of(w0, Q_TILE), n_w), :], q_rot, _NT,
                          preferred_element_type=F32)
    dpos = q_pos - (w0 + lax.broadcasted_iota(jnp.int32, (n_w, 1), 0))
    m_w = jnp.logical_and(dpos >= 0, dpos < NSA_WINDOW)
    p_w = _softmax_cols(s_w, m_w).astype(BF16)
    o_w = jnp.zeros((HEAD_DIM, cols), F32)
    for i in range(n_w // Q_TILE):
        o_w = o_w + jnp.dot(vwt_ref[0, w0 // Q_TILE + i], p_w[i * Q_TILE:(i + 1) * Q_TILE],
                            preferred_element_type=F32)

    def gate(b):
        gb = gate_ref[0, b]
        return jnp.concatenate([gb[r:r + 1, :] for r in range(NSA_GROUP)], axis=1)

    o_t = gate(0) * o_c + gate(1) * o_s + gate(2) * o_w
    for pr in range(NSA_GROUP // 2):
        two = jnp.concatenate([o_t[:, (2 * pr) * Q_TILE:(2 * pr + 1) * Q_TILE],
                               o_t[:, (2 * pr + 1) * Q_TILE:(2 * pr + 2) * Q_TILE]], axis=0)
        o_ref[:, pr * LANES:(pr + 1) * LANES] = two.T


def _nsa_prompt_attend(q_raw, q_rot, gate_t, kc_vc, ks_h, vs_t, kw_h, vw_t):
    t = q_raw.shape[0]
    assert t % (SEL_HALVES * SEL_HALF) == 0 and t >= NSA_WINDOW + Q_TILE
    nq = t // Q_TILE
    n_cmp_pad = kc_vc.shape[0]
    n_sel = t // SEL_BLOCK
    heads = lambda a: a.reshape(a.shape[0], NSA_KV_HEADS, HEAD_DIM).transpose(1, 0, 2)
    kc_h = heads(kc_vc[:, 0:256].astype(BF16))
    vc_t = heads(kc_vc[:, 256:512].astype(BF16)).transpose(0, 2, 1)
    gate4 = gate_t.reshape(NSA_KV_HEADS, 4, 8, t)
    per_head = lambda a: pl.BlockSpec((1,) + a.shape[1:], lambda g, i: (g,) + (0,) * (a.ndim - 1))
    qspec = pl.BlockSpec((Q_TILE, NSA_GROUP * HEAD_DIM), lambda g, i: (i, g))
    return pl.pallas_call(
        functools.partial(_nsa_prompt_kernel, n_cmp_pad=n_cmp_pad, n_sel=n_sel),
        grid=(NSA_KV_HEADS, nq),
        in_specs=[qspec, qspec, per_head(kc_h), per_head(vc_t), per_head(ks_h), per_head(vs_t),
                  per_head(kw_h), per_head(vw_t),
                  pl.BlockSpec((1, 4, 8, Q_TILE), lambda g, i: (g, 0, 0, i))],
        out_specs=qspec,
        out_shape=jax.ShapeDtypeStruct((t, 1024), F32),
        scratch_shapes=[pltpu.VMEM((n_cmp_pad + 16, Q_TILE), F32),
                        pltpu.VMEM((-(-n_sel // HEAD_DIM), NSA_GROUP * Q_TILE, LANES), BF16),
                        pltpu.VMEM((SEL_HALVES, SEL_HALF, NSA_GROUP * Q_TILE), F32),
                        pltpu.VMEM((SEL_HALVES, SEL_HALF, NSA_GROUP * Q_TILE), F32)],
        compiler_params=_cparams("parallel", "arbitrary"),
        name="nsa_prompt_attn",
    )(q_raw, q_rot, kc_h, vc_t, ks_h, vs_t, kw_h, vw_t, gate4)


SMP_PAGES = 16
BLOCKS_PER_STEP = SMP_PAGES * PAGE_SIZE // SEL_BLOCK


def _nsa_sample_kernel(tbl_ref, *refs, n_cmp_pad, n_sel, n_sel_pad, n_past, nt):
    del tbl_ref
    pages = refs[:SMP_PAGES]
    (qraw_ref, qrot_ref, kcvc_ref, knew_ref, wst_ref, wnew_ref, gate_ref, spread_ref, o_ref, wout_ref,
     imp_ref, bias2_ref, m_ref, l_ref, acc_ref, oc_ref) = refs[SMP_PAGES:]
    si = pl.program_id(1)
    q_rot = qrot_ref[0]
    lane = lax.broadcasted_iota(jnp.int32, (LANES, LANES), 1)
    t_row = lax.broadcasted_iota(jnp.int32, (LANES, 1), 0) % nt
    new0 = LANES - nt

    @pl.when(si == 0)
    def _():
        t_col = lax.broadcasted_iota(jnp.int32, (1, LANES), 1) % nt
        q_pos = n_past + t_col
        kc = kcvc_ref[0, :, 0:256].astype(BF16)
        s_c = lax.dot_general(kc, qraw_ref[0], _NT, preferred_element_type=F32)
        c_end = lax.broadcasted_iota(jnp.int32, (n_cmp_pad, 1), 0) * CMP_STRIDE + (CMP_BLOCK - 1)
        p_c = _softmax_cols(s_c, jnp.logical_and(c_end <= q_pos, c_end < n_past))
        o_c = jnp.zeros((LANES, NSA_KV_WIDTH), F32)
        for c in range(n_cmp_pad // LANES):
            rs = slice(c * LANES, (c + 1) * LANES)
            o_c = o_c + jnp.dot(p_c[rs].T.astype(BF16), kcvc_ref[0, rs, 256:512].astype(BF16),
                                preferred_element_type=F32)
        oc_ref[...] = o_c
        imp = p_c
        for r in range(1, NSA_GROUP):
            imp = imp + pltpu.roll(p_c, r * (LANES // NSA_GROUP), 1)
        imp_ref[0:8, :] = jnp.zeros((8, LANES), F32)
        imp_ref[8:8 + n_cmp_pad, :] = imp
        imp_ref[8 + n_cmp_pad:, :] = jnp.zeros((imp_ref.shape[0] - 8 - n_cmp_pad, LANES), F32)
        bias_t = _select_bias(imp_ref, n_sel, n_sel_pad, q_pos)
        n_grp = bias2_ref.shape[0]
        pad_rows = n_grp * BLOCKS_PER_STEP - n_sel_pad
        bias_t = jnp.concatenate([bias_t, jnp.full((pad_rows, LANES), NEG_INF, F32)], axis=0)
        fill = jnp.zeros((LANES - BLOCKS_PER_STEP, LANES), F32)
        for s in range(n_grp):
            grp = bias_t[s * BLOCKS_PER_STEP:(s + 1) * BLOCKS_PER_STEP]
            bias2_ref[s] = jnp.concatenate([grp, fill], axis=0).T
        m_ref[...] = jnp.full(m_ref.shape, NEG_INF, F32)
        l_ref[...] = jnp.zeros(l_ref.shape, F32)
        acc_ref[...] = jnp.zeros_like(acc_ref)

    def online(s_parts, v_parts):
        m_old = m_ref[:, 0:1]
        top = s_parts[0]
        for s in s_parts[1:]:
            top = jnp.maximum(top, s)
        m_new = jnp.maximum(m_old, jnp.max(top, axis=1, keepdims=True))
        alpha = jnp.exp(m_old - m_new)
        acc = alpha * acc_ref[...]
        p_sum = None
        for s, v in zip(s_parts, v_parts):
            p = jnp.exp(s - m_new)
            p_sum = p if p_sum is None else p_sum + p
            acc = acc + lax.dot_general(p.astype(BF16), v, _NT, preferred_element_type=F32)
        l_new = alpha * l_ref[:, 0:1] + jnp.sum(p_sum, axis=1, keepdims=True)
        m_ref[...] = jnp.broadcast_to(m_new, m_ref.shape)
        l_ref[...] = jnp.broadcast_to(l_new, l_ref.shape)
        acc_ref[...] = acc

    b2 = bias2_ref[si].astype(BF16)
    s_parts, v_parts = [], []
    for j, p in enumerate(pages):
        kt = p[0, 0].reshape(NSA_KV_WIDTH, PAGE_SIZE).astype(BF16)
        s = (jnp.dot(q_rot, kt, preferred_element_type=F32)
             + jnp.dot(b2, spread_ref[:, j * PAGE_SIZE:(j + 1) * PAGE_SIZE], preferred_element_type=F32))
        s_parts.append(s)
        v_parts.append(p[0, 1].reshape(NSA_KV_WIDTH, PAGE_SIZE).astype(BF16))
    online(s_parts, v_parts)

    @pl.when(si == pl.num_programs(1) - 1)
    def _():
        s_n = jnp.dot(q_rot, knew_ref[0, 0].astype(BF16), preferred_element_type=F32)
        last = bias2_ref[n_past // SEL_BLOCK // BLOCKS_PER_STEP]
        lb = (n_past // SEL_BLOCK) % BLOCKS_PER_STEP
        ok = jnp.logical_and(lane <= t_row, lane < nt)
        online([jnp.where(ok, s_n + last[:, lb:lb + 1], NEG_INF)], [knew_ref[0, 1].astype(BF16)])
        o_s = acc_ref[...] * (1.0 / l_ref[:, 0:1])

        n_buf = wst_ref.shape[-1]
        kwt, vwt = wst_ref[0, 0].reshape(NSA_KV_WIDTH, n_buf), wst_ref[0, 1].reshape(NSA_KV_WIDTH, n_buf)
        kwn, vwn = wnew_ref[0, 0], wnew_ref[0, 1]
        s_w = jnp.dot(q_rot, kwt.astype(BF16), preferred_element_type=F32)
        s_wn = jnp.dot(q_rot, kwn.astype(BF16), preferred_element_type=F32)
        dpos = n_buf + t_row - lax.broadcasted_iota(jnp.int32, (1, n_buf), 1)
        dpos_n = t_row - (lane - new0)
        p_w, p_wn, _ = _softmax_rows2(s_w, jnp.logical_and(dpos >= 0, dpos < NSA_WINDOW),
                                      s_wn, jnp.logical_and(lane >= new0, dpos_n >= 0))
        o_w = (lax.dot_general(p_w.astype(BF16), vwt.astype(BF16), _NT, preferred_element_type=F32)
               + lax.dot_general(p_wn.astype(BF16), vwn.astype(BF16), _NT, preferred_element_type=F32))
        g = gate_ref[0]
        o_ref[0] = g[:, 0:1] * oc_ref[...] + g[:, 1:2] * o_s + g[:, 2:3] * o_w

        is_new = lax.broadcasted_iota(jnp.int32, (NSA_KV_WIDTH, LANES), 1) >= new0
        for st, (old, new) in enumerate(((kwt, kwn), (vwt, vwn))):
            rolled = pltpu.roll(old, n_buf - nt, 1)
            tail = jnp.where(is_new, new, rolled[:, n_buf - LANES:])
            full = jnp.concatenate([rolled[:, :n_buf - LANES], tail], axis=1)
            wout_ref[0, st] = full.reshape(NSA_KV_HEADS, HEAD_DIM, n_buf)


def _nsa_sample_attend(q_raw, q_rot, gate_t, kc_vc, knew, cache_t, page_table, win_t, wnew):
    b, nt = q_raw.shape[0], q_raw.shape[1]
    npg = page_table.shape[1]
    n_past = npg * PAGE_SIZE
    n_cmp_pad = kc_vc.shape[1]
    n_sel = -(-(n_past + nt) // SEL_BLOCK)
    n_sel_pad = -(-n_sel // 8) * 8
    n_grp = -(-n_sel_pad // BLOCKS_PER_STEP)
    eye = jnp.eye(NSA_KV_HEADS, dtype=BF16)

    def blockdiag(q):
        q5 = q.reshape(b, nt, NSA_KV_HEADS, NSA_GROUP, HEAD_DIM).transpose(0, 3, 2, 1, 4)
        q6 = q5[:, :, :, :, None, :] * eye[None, None, :, None, :, None]
        return q6.reshape(b, LANES, NSA_KV_WIDTH)

    gt = gate_t.reshape(NSA_KV_HEADS, 4, 8, b, nt)[:, :3, :NSA_GROUP].transpose(3, 2, 0, 4, 1)
    gt = jnp.pad(gt.reshape(b, LANES, 3), ((0, 0), (0, 0), (0, LANES - 3)))
    flat = page_table.reshape(-1)

    def page_spec(j):
        return pl.BlockSpec((1, 2, NSA_KV_HEADS, HEAD_DIM, PAGE_SIZE),
                            lambda bi, si, tbl: (tbl[bi * npg + si * SMP_PAGES + j], 1, 0, 0, 0))

    per_b = lambda a: pl.BlockSpec((1,) + a.shape[1:], lambda bi, si, tbl: (bi,) + (0,) * (a.ndim - 1))
    qr, qo = blockdiag(q_raw), blockdiag(q_rot)
    spread = (jnp.arange(LANES)[:, None] == jnp.arange(SMP_PAGES * PAGE_SIZE)[None, :] // SEL_BLOCK).astype(BF16)
    o, wout = pl.pallas_call(
        functools.partial(_nsa_sample_kernel, n_cmp_pad=n_cmp_pad, n_sel=n_sel, n_sel_pad=n_sel_pad,
                          n_past=n_past, nt=nt),
        grid_spec=pltpu.PrefetchScalarGridSpec(
            num_scalar_prefetch=1, grid=(b, npg // SMP_PAGES),
            in_specs=[page_spec(j) for j in range(SMP_PAGES)] + [
                per_b(qr), per_b(qo), per_b(kc_vc), per_b(knew), per_b(win_t), per_b(wnew), per_b(gt),
                pl.BlockSpec(spread.shape, lambda bi, si, tbl: (0, 0))],
            out_specs=[pl.BlockSpec((1, LANES, NSA_KV_WIDTH), lambda bi, si, tbl: (bi, 0, 0)), per_b(win_t)],
            scratch_shapes=[pltpu.VMEM((4 * n_sel_pad + 16, LANES), F32),
                            pltpu.VMEM((n_grp, LANES, LANES), F32),
                            pltpu.VMEM((LANES, LANES), F32), pltpu.VMEM((LANES, LANES), F32),
                            pltpu.VMEM((LANES, NSA_KV_WIDTH), F32), pltpu.VMEM((LANES, NSA_KV_WIDTH), F32)]),
        out_shape=[jax.ShapeDtypeStruct((b, LANES, NSA_KV_WIDTH), F32), jax.ShapeDtypeStruct(win_t.shape, F32)],
        compiler_params=_cparams("parallel", "arbitrary"),
        name="nsa_sample_attn",
    )(flat, *([cache_t] * SMP_PAGES), qr, qo, kc_vc, knew, win_t, wnew, gt, spread)
    o6 = o.reshape(b, NSA_GROUP, NSA_KV_HEADS, nt, NSA_KV_HEADS, HEAD_DIM)
    o5 = jnp.stack([o6[:, :, gg, :, gg, :] for gg in range(NSA_KV_HEADS)], axis=2)
    return o5.transpose(0, 3, 2, 1, 4).reshape(b, nt, 1024), wout


def _rows_last(a):
    n = a.ndim
    return a.transpose(tuple(range(n - 4)) + (n - 3, n - 2, n - 1, n - 4))


def _rows_first(a):
    n = a.ndim
    return a.transpose(tuple(range(n - 4)) + (n - 1, n - 4, n - 3, n - 2))


def _nsa_prompt_layer(x, g, w_in, cmp, w_out):
    t = x.shape[1]
    x2 = x.reshape(t, D_MODEL)
    (q_raw, q_rot, rows_t, cmp_src, ks_h, vs_t, win_t, kw_h, vw_t, sz, gate_t) = _nsa_project(
        x2, g, w_in, jnp.arange(t, dtype=jnp.int32))
    npg = t // PAGE_SIZE
    table = jnp.arange(npg, dtype=jnp.int32).reshape(1, npg)
    kc_vc = _compress(cmp_src.reshape(npg, PAGE_SIZE, 512), table, *cmp, native=False)[0]
    o = _nsa_prompt_attend(q_raw, q_rot, gate_t, kc_vc, ks_h, vs_t, kw_h, vw_t)
    y = _nsa_out(o, sz, x2, w_out, 512)
    keep = min(NSA_WINDOW, t)
    rows = _rows_first(rows_t.reshape(1, 1, 4, NSA_KV_HEADS, HEAD_DIM, t))
    win = _rows_first(win_t[:, t - keep:].reshape(1, 1, 2, NSA_KV_HEADS, HEAD_DIM, keep))
    return y.reshape(1, t, D_MODEL), rows, win


def _nsa_sample_layer(x, cache_l, page_table, win_buf, g, w_in, cmp, w_out):
    b, nt, _ = x.shape
    m = b * nt
    n_past = page_table.shape[1] * PAGE_SIZE
    x2 = x.reshape(m, D_MODEL)
    pos = jnp.tile(n_past + jnp.arange(nt, dtype=jnp.int32), b)
    (q_raw, q_rot, rows_t, _, _, _, win_t, _, _, sz, gate_t) = _nsa_project(x2, g, w_in, pos)
    cache_t = _rows_last(cache_l)
    kc_vc = _compress(cache_t, page_table, *cmp, native=True)
    rows_b = rows_t.reshape(4, NSA_KV_WIDTH, b, nt).transpose(2, 0, 1, 3)
    knew = jnp.pad(rows_b[:, 2:4], ((0, 0), (0, 0), (0, 0), (0, LANES - nt)))
    wnew = jnp.pad(win_t.reshape(2, NSA_KV_WIDTH, b, nt).transpose(2, 0, 1, 3),
                   ((0, 0), (0, 0), (0, 0), (LANES - nt, 0)))
    o, wout = _nsa_sample_attend(q_raw.reshape(b, nt, 1024), q_rot.reshape(b, nt, 1024), gate_t, kc_vc,
                                 knew, cache_t, page_table, _rows_last(win_buf), wnew)
    y = _nsa_out(o.reshape(m, 1024), sz, x2, w_out, m)
    rows = rows_t.T.reshape(b, nt, 4, NSA_KV_HEADS, HEAD_DIM)
    return y.reshape(b, nt, D_MODEL), rows, _rows_first(wout)


def _dil_prompt_layer(x, g, w_in, w_out, gf):
    t = x.shape[1]
    x2 = x.reshape(t, D_MODEL)
    keeps = [min(w, t) for w, _ in DIL_PAIRS]
    proj = _dil_project(x2, g, w_in, jnp.arange(t, dtype=jnp.int32), keeps, with_kvb=True)
    outs, lses, new = [], [], []
    for (q, kvb, kv_t, *_), (w, d), keep in zip(proj, DIL_PAIRS, keeps):
        o, lse = _dil_prompt_attend(q, kvb, d)
        outs.append(o)
        lses.append(lse)
        new.append(_rows_first(kv_t.reshape(1, 1, 2, DIL_HEADS, HEAD_DIM, keep)))
    y = _dil_out(outs, lses, proj[-1][3], x2, w_out, gf, 256)
    return y.reshape(1, t, D_MODEL), new


def _dil_sample_layer(x, bufs, n_past, g, w_in, w_out, gf):
    b, nt, _ = x.shape
    m = b * nt
    x2 = x.reshape(m, D_MODEL)
    pos = jnp.tile(n_past + jnp.arange(nt, dtype=jnp.int32), b)
    proj = _dil_project(x2, g, w_in, pos, [m] * len(DIL_PAIRS), with_kvb=False)
    eye = jnp.eye(DS_HEADS, dtype=BF16)
    nc = DIL_HEADS // DS_HEADS
    outs, lses, new = [], [], []
    for (q, kv_t, *_), buf, (w, d) in zip(proj, bufs, DIL_PAIRS):
        newp = kv_t.reshape(2, DIL_HEADS, HEAD_DIM, b, nt).transpose(3, 0, 1, 2, 4)
        newp = jnp.pad(newp, ((0, 0),) * 4 + ((LANES - nt, 0),))
        q5 = q.reshape(b, nt, nc, DS_HEADS, HEAD_DIM).transpose(0, 2, 3, 1, 4)
        qbd = q5[:, :, :, :, None, :] * eye[None, None, :, None, :, None]
        qbd = qbd.reshape(b, nc, DS_HEADS * nt, DS_HEADS * HEAD_DIM)
        ost, o, lse = _dil_sample_attend(_rows_last(buf), newp, qbd, d)
        outs.append(o.reshape(m, 1024))
        lses.append(lse.reshape(m, 1024))
        new.append(_rows_first(ost))
    y = _dil_out(outs, lses, proj[-1][2], x2, w_out, gf, m)
    return y.reshape(b, nt, D_MODEL), new


def kernel(x_prompt, x_sample, cache_nsa_kv, state_nsa_win, state_dil_g0, state_dil_g1, state_dil_g2,
           page_table, nsa_norm, nsa_w_in, nsa_cmp_w1, nsa_cmp_b1, nsa_cmp_w2, nsa_cmp_pos, nsa_w_out,
           dil_norm, dil_w_in, dil_w_out, final_norm):
    n_past = page_table.shape[1] * PAGE_SIZE
    cmp = (nsa_cmp_w1[0], nsa_cmp_b1[0], nsa_cmp_w2[0], nsa_cmp_pos[0])
    y_p, rows_p, win_p = _nsa_prompt_layer(x_prompt, nsa_norm[0], nsa_w_in[0], cmp, nsa_w_out[0])
    y_s, rows_s, win_s = _nsa_sample_layer(x_sample, cache_nsa_kv[0], page_table, state_nsa_win[0],
                                           nsa_norm[0], nsa_w_in[0], cmp, nsa_w_out[0])
    y_p, new_p = _dil_prompt_layer(y_p, dil_norm[0], dil_w_in[0], dil_w_out[0], final_norm)
    y_s, new_s = _dil_sample_layer(y_s, (state_dil_g0[0], state_dil_g1[0], state_dil_g2[0]), n_past,
                                   dil_norm[0], dil_w_in[0], dil_w_out[0], final_norm)
    add = lambda a: a[None]
    return (y_p, y_s, rows_p, add(rows_s), win_p, add(win_s),
            new_p[0], add(new_s[0]), new_p[1], add(new_s[1]), new_p[2], add(new_s[2]))
```

```python
import functools

import jax
import jax.numpy as jnp
from jax import lax
from jax.experimental import pallas as pl
from jax.experimental.pallas import tpu as pltpu

F32 = jnp.float32
BF16 = jnp.bfloat16

D_MODEL = 1024
HEAD_DIM = 64
ROT_DIM = HEAD_DIM // 4
ROPE_THETA = 500000.0
NORM_EPS = 1e-6
NEG_INF = -1e30
Q_SCALE = HEAD_DIM ** -0.5
LOG2_E = 1.4426950408889634
PAGE_SIZE = 128

NSA_KV_HEADS = 4
NSA_GROUP = 4
NSA_KV_WIDTH = NSA_KV_HEADS * HEAD_DIM
CMP_BLOCK = 32
CMP_STRIDE = 16
SEL_BLOCK = 64
SEL_TOP_N = 16
NSA_WINDOW = 512
FORCE_BONUS = 1e4
DIL_PAIRS = ((128, 1), (512, 4), (2048, 16))
DIL_HEADS = 16

LANES = 128
Q_TILE = 128
SEL_HALF = 256
SEL_HALVES = 4
PROJ_TILE = 256
VMEM_LIMIT = 56 * 1024 * 1024

_NT = (((1,), (1,)), ((), ()))


def _cparams(*sem):
    return pltpu.CompilerParams(dimension_semantics=sem, vmem_limit_bytes=VMEM_LIMIT)


def _rope_tables(pos):
    half = ROT_DIM // 2
    inv = ROPE_THETA ** (-jnp.arange(half, dtype=F32) / half)
    ang = pos.astype(F32)[:, None] * inv[None, :]
    cos, sin = jnp.cos(ang), jnp.sin(ang)
    m = pos.shape[0]
    one, zero, z8 = jnp.ones((m, 48), F32), jnp.zeros((m, 48), F32), jnp.zeros((m, 8), F32)
    c = jnp.concatenate([cos, cos, one], axis=1)
    s1 = jnp.concatenate([-sin, z8, zero], axis=1)
    s2 = jnp.concatenate([z8, sin, zero], axis=1)
    return jnp.concatenate([c, c, s1, s1, s2, s2], axis=1), cos.T, sin.T


def _rope(y, tab_ref):
    c, s1, s2 = tab_ref[:, 0:128], tab_ref[:, 128:256], tab_ref[:, 256:384]
    out = []
    for j in range(y.shape[1] // LANES):
        yc = y[:, j * LANES:(j + 1) * LANES]
        out.append(yc * c + pltpu.roll(yc, LANES - 8, 1) * s1 + pltpu.roll(yc, 8, 1) * s2)
    return out[0] if len(out) == 1 else jnp.concatenate(out, axis=1)


def _rope_t(y, cos, sin):
    out = []
    for h in range(y.shape[0] // HEAD_DIM):
        b = h * HEAD_DIM
        x1, x2 = y[b:b + 8], y[b + 8:b + 16]
        out += [x1 * cos - x2 * sin, x2 * cos + x1 * sin, y[b + 16:b + HEAD_DIM]]
    return jnp.concatenate(out, axis=0)


def _normed_bf16(x_ref, g_ref):
    x = x_ref[...]
    ms = jnp.mean(x * x, axis=-1, keepdims=True)
    return ((x * lax.rsqrt(ms + NORM_EPS)) * g_ref[...]).astype(BF16)


def _silu(z):
    return z * jax.nn.sigmoid(z)


def _softmax_cols(s, mask):
    s = jnp.where(mask, s, NEG_INF)
    m = jnp.max(s, axis=0, keepdims=True)
    e = jnp.exp(s - m)
    den = jnp.sum(e, axis=0, keepdims=True)
    return e * jnp.where(m > 0.5 * NEG_INF, 1.0 / den, 0.0)


def _softmax_rows2(s_a, m_a, s_b, m_b):
    s_a = jnp.where(m_a, s_a, NEG_INF)
    s_b = jnp.where(m_b, s_b, NEG_INF)
    m = jnp.maximum(jnp.max(s_a, axis=1, keepdims=True), jnp.max(s_b, axis=1, keepdims=True))
    e_a = jnp.where(m_a, jnp.exp(s_a - m), 0.0)
    e_b = jnp.where(m_b, jnp.exp(s_b - m), 0.0)
    den = jnp.sum(e_a, axis=1, keepdims=True) + jnp.sum(e_b, axis=1, keepdims=True)
    inv = 1.0 / jnp.maximum(den, 1e-30)
    return e_a * inv, e_b * inv, m + jnp.log(den)


def _nsa_proj_kernel(x_ref, g_ref, wnn_ref, wt_ref, tab_ref, cos_ref, sin_ref,
                     qraw_ref, qrot_ref, rowst_ref, cmp_ref, ks_ref, vst_ref, wint_ref, kw_ref, vwt_ref,
                     sz_ref, gatet_ref):
    tm = x_ref.shape[0]
    hb = _normed_bf16(x_ref, g_ref)
    nn = lambda a, b: jnp.dot(hb, wnn_ref[:, a:b], preferred_element_type=F32)
    nt = lambda a, b: lax.dot_general(wt_ref[a:b, :], hb, _NT, preferred_element_type=F32)
    cos, sin = cos_ref[...], sin_ref[...]
    q = nn(0, 1024) * Q_SCALE
    qraw_ref[...] = q.astype(BF16)
    qrot_ref[...] = _rope(q, tab_ref).astype(BF16)
    kcs = nn(1024, 1792)
    cmp_ref[...] = kcs[:, :512]
    ks = _rope(kcs[:, 512:768], tab_ref)
    kw = _rope(nn(1792, 2048), tab_ref)
    row = pl.program_id(0) * tm + lax.broadcasted_iota(jnp.int32, (tm, LANES), 0)
    lane = lax.broadcasted_iota(jnp.int32, (tm, LANES), 1)
    mark = jnp.where(lane - HEAD_DIM == (row // SEL_BLOCK) % HEAD_DIM, 1.0, 0.0)
    for g in range(NSA_KV_HEADS):
        pair = ks[:, (g // 2) * LANES:(g // 2 + 1) * LANES]
        if g % 2:
            pair = pltpu.roll(pair, HEAD_DIM, 1)
        ks_ref[g] = jnp.where(lane < HEAD_DIM, pair, mark).astype(BF16)
        kw_ref[g] = kw[:, g * HEAD_DIM:(g + 1) * HEAD_DIM].astype(BF16)
    sz_ref[...] = _silu(nn(2048, 3072))
    rt = nt(0, 1024)
    rowst_ref[...] = jnp.concatenate([rt[:512], _rope_t(rt[512:768], cos, sin), rt[768:]], axis=0)
    vs = rt[768:1024].astype(BF16)
    wt = nt(1024, 1536)
    wint_ref[...] = jnp.concatenate([_rope_t(wt[:256], cos, sin), wt[256:]], axis=0)
    vw = wt[256:512].astype(BF16)
    for g in range(NSA_KV_HEADS):
        hs = slice(g * HEAD_DIM, (g + 1) * HEAD_DIM)
        for c in range(tm // SEL_HALF):
            vst_ref[g, c] = vs[hs, c * SEL_HALF:(c + 1) * SEL_HALF]
        for c in range(tm // Q_TILE):
            vwt_ref[g, c] = vw[hs, c * Q_TILE:(c + 1) * Q_TILE]
    gatet_ref[...] = jax.nn.sigmoid(nt(1536, 1664))


def _nsa_project(x2, g, w_in, pos):
    m = x2.shape[0]
    tm = PROJ_TILE
    w_nn = jnp.concatenate([w_in[:, 0:1024], w_in[:, 1024:1792], w_in[:, 2048:2304], w_in[:, 2608:3632]],
                           axis=1).astype(BF16)
    gl = w_in[:, 2560:2608].T.reshape(NSA_KV_HEADS, NSA_GROUP, 3, D_MODEL).transpose(0, 2, 1, 3)
    gl = jnp.pad(gl, ((0, 0), (0, 1), (0, 8 - NSA_GROUP), (0, 0))).reshape(LANES, D_MODEL)
    w_t = jnp.concatenate([w_in[:, 1024:2048].T, w_in[:, 2048:2560].T, gl], axis=0).astype(BF16)
    tab, cos_t, sin_t = _rope_tables(pos)
    g2 = g.reshape(1, D_MODEL)
    row = lambda n: pl.BlockSpec((tm, n), lambda i: (i, 0))
    col = lambda n: pl.BlockSpec((n, tm), lambda i: (0, i))
    full = lambda a: pl.BlockSpec(a.shape, lambda i: (0,) * a.ndim)
    out = [
        (row(1024), (m, 1024), BF16), (row(1024), (m, 1024), BF16),
        (col(1024), (1024, m), F32), (row(512), (m, 512), F32),
        (pl.BlockSpec((NSA_KV_HEADS, tm, LANES), lambda i: (0, i, 0)), (NSA_KV_HEADS, m, LANES), BF16),
        (pl.BlockSpec((NSA_KV_HEADS, tm // SEL_HALF, HEAD_DIM, SEL_HALF), lambda i: (0, i, 0, 0)),
         (NSA_KV_HEADS, m // SEL_HALF, HEAD_DIM, SEL_HALF), BF16),
        (col(512), (512, m), F32),
        (pl.BlockSpec((NSA_KV_HEADS, tm, HEAD_DIM), lambda i: (0, i, 0)), (NSA_KV_HEADS, m, HEAD_DIM), BF16),
        (pl.BlockSpec((NSA_KV_HEADS, tm // Q_TILE, HEAD_DIM, Q_TILE), lambda i: (0, i, 0, 0)),
         (NSA_KV_HEADS, m // Q_TILE, HEAD_DIM, Q_TILE), BF16),
        (row(1024), (m, 1024), F32), (col(LANES), (LANES, m), F32),
    ]
    return pl.pallas_call(
        _nsa_proj_kernel,
        grid=(m // tm,),
        in_specs=[row(D_MODEL), full(g2), full(w_nn), full(w_t), row(384), col(8), col(8)],
        out_specs=[o[0] for o in out],
        out_shape=[jax.ShapeDtypeStruct(o[1], o[2]) for o in out],
        compiler_params=_cparams("parallel"),
        name="nsa_proj",
    )(x2, g2, w_nn, w_t, tab, cos_t, sin_t)


def _dil_proj_kernel(x_ref, g_ref, wnn_ref, wt_ref, tab_ref, cos_ref, sin_ref, perm_ref, *out_refs,
                     with_kvb, with_z, tail_steps, tail_lanes):
    out_refs = list(out_refs)
    q_ref = out_refs.pop(0)
    kvb_ref = out_refs.pop(0) if with_kvb else None
    kvt_ref = out_refs.pop(0)
    sz_ref = out_refs.pop(0) if with_z else None
    tm = x_ref.shape[0]
    hb = _normed_bf16(x_ref, g_ref)
    nn = lambda a, b: jnp.dot(hb, wnn_ref[:, a:b], preferred_element_type=F32)

    def put(ref, cols, y):
        y = y.astype(BF16)
        if len(ref.shape) == 3:
            y = jnp.dot(perm_ref[...], y, preferred_element_type=F32).astype(BF16)
            ref[:, :, cols] = y.reshape(ref.shape[0], ref.shape[1], y.shape[1])
        else:
            ref[:, cols] = y

    put(q_ref, slice(0, 1024), _rope(nn(0, 1024) * Q_SCALE, tab_ref))
    if with_kvb:
        put(kvb_ref, slice(0, 1024), _rope(nn(1024, 2048), tab_ref))
        put(kvb_ref, slice(1024, 2048), nn(2048, 3072))
    if with_z:
        sz_ref[...] = _silu(nn(3072, 4096))

    @pl.when(pl.program_id(0) >= pl.num_programs(0) - tail_steps)
    def _():
        kt = lax.dot_general(wt_ref[0:1024, :], hb, _NT, preferred_element_type=F32)
        kt = _rope_t(kt, cos_ref[...], sin_ref[...])
        vt = lax.dot_general(wt_ref[1024:2048, :], hb, _NT, preferred_element_type=F32)
        kvt_ref[0:1024, :] = kt[:, tm - tail_lanes:]
        kvt_ref[1024:2048, :] = vt[:, tm - tail_lanes:]


def _dil_project(x2, g, w_in, pos, keeps, dils, with_kvb):
    m = x2.shape[0]
    tm = PROJ_TILE
    n = m // tm
    tab, cos_t, sin_t = _rope_tables(pos)
    g2 = g.reshape(1, D_MODEL)
    row = lambda c: pl.BlockSpec((tm, c), lambda i: (i, 0))
    col = lambda c: pl.BlockSpec((c, tm), lambda i: (0, i))
    full = lambda a: pl.BlockSpec(a.shape, lambda i: (0,) * a.ndim)
    outs = []
    for gi, (keep, dil) in enumerate(zip(keeps, dils)):
        last = gi == len(keeps) - 1

        def rows_out(c):
            if dil == 1:
                return row(c), (m, c)
            return pl.BlockSpec((dil, tm // dil, c), lambda i: (0, i, 0)), (dil, m // dil, c)

        o_row = jnp.arange(tm)
        perm = ((o_row % (tm // dil)) * dil + o_row // (tm // dil))[:, None] == jnp.arange(tm)[None, :]
        perm = perm.astype(BF16)
        wg = w_in[:, 3072 * gi:3072 * (gi + 1)]
        cols = [wg[:, :1024]] + ([wg[:, 1024:]] if with_kvb else []) + ([w_in[:, 9216:]] if last else [])
        if not with_kvb and last:
            cols.insert(1, jnp.zeros((D_MODEL, 2048), F32))
        w_nn = jnp.concatenate(cols, axis=1).astype(BF16)
        w_t = wg[:, 1024:].T.astype(BF16)
        tail_steps = max(keep // tm, 1)
        tail_lanes = min(keep, tm)
        spec = [rows_out(1024) + (BF16,)]
        if with_kvb:
            spec.append(rows_out(2048) + (BF16,))
        spec.append((pl.BlockSpec((2048, tail_lanes), lambda i: (0, jnp.maximum(i - (n - tail_steps), 0))),
                     (2048, keep), F32))
        if last:
            spec.append((row(1024), (m, 1024), F32))
        outs.append(pl.pallas_call(
            functools.partial(_dil_proj_kernel, with_kvb=with_kvb, with_z=last, tail_steps=tail_steps,
                              tail_lanes=tail_lanes),
            grid=(n,),
            in_specs=[row(D_MODEL), full(g2), full(w_nn), full(w_t), row(384), col(8), col(8), full(perm)],
            out_specs=[s[0] for s in spec],
            out_shape=[jax.ShapeDtypeStruct(s[1], s[2]) for s in spec],
            compiler_params=_cparams("arbitrary"),
            name=f"dil_proj_g{gi}",
        )(x2, g2, w_nn, w_t, tab, cos_t, sin_t, perm))
    return outs


def _nsa_out_kernel(o_ref, sz_ref, x_ref, w_ref, y_ref):
    a = (o_ref[...] * sz_ref[...]).astype(BF16)
    y_ref[...] = x_ref[...] + jnp.dot(a, w_ref[...], preferred_element_type=F32)


def _dil_out_kernel(o0_ref, o1_ref, o2_ref, l0_ref, l1_ref, l2_ref, sz_ref, x_ref, w_ref, gf_ref, y_ref):
    l0, l1, l2 = l0_ref[...], l1_ref[...], l2_ref[...]
    m = jnp.maximum(jnp.maximum(l0, l1), l2)
    e0, e1, e2 = jnp.exp(l0 - m), jnp.exp(l1 - m), jnp.exp(l2 - m)
    inv = 1.0 / (e0 + e1 + e2)
    o = (e0 * inv) * o0_ref[...] + (e1 * inv) * o1_ref[...] + (e2 * inv) * o2_ref[...]
    a = (o * sz_ref[...]).astype(BF16)
    y = x_ref[...] + jnp.dot(a, w_ref[...], preferred_element_type=F32)
    ms = jnp.mean(y * y, axis=-1, keepdims=True)
    y_ref[...] = (y * lax.rsqrt(ms + NORM_EPS)) * gf_ref[...]


def _nsa_out(o, sz, x2, w_out, tm):
    m = x2.shape[0]
    row = pl.BlockSpec((tm, D_MODEL), lambda i: (i, 0))
    w = w_out.astype(BF16)
    return pl.pallas_call(
        _nsa_out_kernel, grid=(m // tm,),
        in_specs=[row, row, row, pl.BlockSpec(w.shape, lambda i: (0, 0))],
        out_specs=row, out_shape=jax.ShapeDtypeStruct((m, D_MODEL), F32),
        compiler_params=_cparams("parallel"), name="nsa_out",
    )(o, sz, x2, w)


def _dil_out(outs, lses, sz, x2, w_out, gf, tm):
    m = x2.shape[0]
    row = pl.BlockSpec((tm, D_MODEL), lambda i: (i, 0))
    w = w_out.astype(BF16)
    gf2 = gf.reshape(1, D_MODEL)
    return pl.pallas_call(
        _dil_out_kernel, grid=(m // tm,),
        in_specs=[row] * 8 + [pl.BlockSpec(w.shape, lambda i: (0, 0)), pl.BlockSpec(gf2.shape, lambda i: (0, 0))],
        out_specs=row, out_shape=jax.ShapeDtypeStruct((m, D_MODEL), F32),
        compiler_params=_cparams("parallel"), name="dil_out",
    )(*outs, *lses, sz, x2, w, gf2)


def _dil_prompt_kernel(q_ref, kp_ref, kc_ref, vp_ref, vc_ref, o_ref, lse_ref):
    ib = pl.program_id(1)
    tq = lax.broadcasted_iota(jnp.int32, (Q_TILE, 2 * Q_TILE), 0)
    kk = lax.broadcasted_iota(jnp.int32, (Q_TILE, 2 * Q_TILE), 1)
    diff = kk - tq
    lo_key = jnp.where(ib > 0, 0, Q_TILE)
    mask = jnp.logical_and(jnp.logical_and(diff >= 0, diff <= Q_TILE), kk >= lo_key)
    lo = lax.broadcasted_iota(jnp.int32, (Q_TILE, LANES), 1) < HEAD_DIM
    for pr in range(DIL_HEADS // 2):
        cs = slice(pr * LANES, (pr + 1) * LANES)
        q2 = q_ref[0, :, cs]
        k2 = jnp.concatenate([kp_ref[0, :, cs], kc_ref[0, :, cs]], axis=0)
        v2 = jnp.concatenate([vp_ref[0, :, cs], vc_ref[0, :, cs]], axis=0)
        outs, lses = [], []
        for half in range(2):
            keep = lo if half == 0 else jnp.logical_not(lo)
            qm = jnp.where(keep, q2, jnp.zeros_like(q2))
            s = lax.dot_general(qm, k2, _NT, preferred_element_type=F32)
            s = jnp.where(mask, s, NEG_INF)
            m = jnp.max(s, axis=-1, keepdims=True)
            e = jnp.where(mask, jnp.exp(s - m), 0.0)
            den = jnp.sum(e, axis=-1, keepdims=True)
            p = e * (1.0 / jnp.maximum(den, 1e-30))
            outs.append(jnp.dot(p.astype(BF16), v2, preferred_element_type=F32))
            lses.append(jnp.broadcast_to(m + jnp.log(den), (Q_TILE, LANES)))
        o_ref[:, cs] = jnp.where(lo, outs[0], outs[1])
        lse_ref[:, cs] = jnp.where(lo, lses[0], lses[1])


def _dil_prompt_attend(q, kvb, d):
    n = q.shape[-2]
    t = n * d
    qv = q.reshape(d, n, 1024)
    kvv = kvb.reshape(d, n, 2048)
    blk = lambda f: pl.BlockSpec((Q_TILE, 1024), f)
    blk3 = lambda f: pl.BlockSpec((1, Q_TILE, 1024), f)
    prev = lambda i: jnp.maximum(i - 1, 0)
    out = pl.pallas_call(
        _dil_prompt_kernel,
        grid=(d, n // Q_TILE),
        in_specs=[blk3(lambda r, i: (r, i, 0)),
                  blk3(lambda r, i: (r, prev(i), 0)), blk3(lambda r, i: (r, i, 0)),
                  blk3(lambda r, i: (r, prev(i), 1)), blk3(lambda r, i: (r, i, 1))],
        out_specs=[blk(lambda r, i: (i, r)), blk(lambda r, i: (i, r))],
        out_shape=[jax.ShapeDtypeStruct((n, d * 1024), F32)] * 2,
        compiler_params=_cparams("parallel", "arbitrary"),
        name=f"dil_prompt_d{d}",
    )(qv, kvv, kvv, kvv, kvv)
    return out[0].reshape(t, 1024), out[1].reshape(t, 1024)


def _ds_heads(w):
    return min(DIL_HEADS, max(4, 8192 // w))


def _dil_sample_kernel(st_ref, newp_ref, qbd_ref, ost_ref, o_ref, lse_ref, *, w, d):
    nt = o_ref.shape[1]
    heads = st_ref.shape[2]
    rows = heads * HEAD_DIM
    nq = heads * nt
    new0 = LANES - nt
    kt, vt = st_ref[0, 0].reshape(rows, w), st_ref[0, 1].reshape(rows, w)
    kn, vn = newp_ref[0, 0].reshape(rows, LANES), newp_ref[0, 1].reshape(rows, LANES)
    qbd = qbd_ref[0, 0]
    s_st = jnp.dot(qbd, kt.astype(BF16), preferred_element_type=F32)
    s_nw = jnp.dot(qbd, kn.astype(BF16), preferred_element_type=F32)
    t_row = lax.broadcasted_iota(jnp.int32, (nq, 1), 0) % nt
    diff_st = (w + t_row) - lax.broadcasted_iota(jnp.int32, (1, w), 1)
    m_st = jnp.logical_and((diff_st & (d - 1)) == 0, diff_st <= w)
    i_nw = lax.broadcasted_iota(jnp.int32, (1, LANES), 1) - new0
    diff_nw = t_row - i_nw
    m_nw = jnp.logical_and(jnp.logical_and(i_nw >= 0, diff_nw >= 0), (diff_nw & (d - 1)) == 0)
    p_st, p_nw, lse = _softmax_rows2(s_st, m_st, s_nw, m_nw)
    acc = (lax.dot_general(p_st.astype(BF16), vt.astype(BF16), _NT, preferred_element_type=F32)
           + lax.dot_general(p_nw.astype(BF16), vn.astype(BF16), _NT, preferred_element_type=F32))
    diag = (lax.broadcasted_iota(jnp.int32, (nq, rows), 0) // nt
            == lax.broadcasted_iota(jnp.int32, (nq, rows), 1) // HEAD_DIM)

    def fold(a):
        a = jnp.where(diag, a, 0.0)
        out = a[0:nt]
        for h in range(1, heads):
            out = out + a[h * nt:(h + 1) * nt]
        return out

    o_ref[0] = fold(acc)
    lse_ref[0] = fold(jnp.broadcast_to(lse, (nq, rows)))
    is_new = lax.broadcasted_iota(jnp.int32, (rows, LANES), 1) >= new0
    for si, (old, new) in enumerate(((kt, kn), (vt, vn))):
        rolled = pltpu.roll(old, w - nt, 1)
        tail = jnp.where(is_new, new, rolled[:, w - LANES:])
        full = tail if w == LANES else jnp.concatenate([rolled[:, :w - LANES], tail], axis=1)
        ost_ref[0, si] = full.reshape(heads, HEAD_DIM, w)


def _dil_sample_attend(state_t, newp, qbd, d):
    b, w = state_t.shape[0], state_t.shape[-1]
    heads = _ds_heads(w)
    nt = qbd.shape[2] // heads
    rows = heads * HEAD_DIM
    st_spec = lambda n: pl.BlockSpec((1, 2, heads, HEAD_DIM, n), lambda i, c: (i, 0, c, 0, 0))
    o_spec = pl.BlockSpec((1, nt, rows), lambda i, c: (i, 0, c))
    return pl.pallas_call(
        functools.partial(_dil_sample_kernel, w=w, d=d),
        grid=(b, DIL_HEADS // heads),
        in_specs=[st_spec(w), st_spec(LANES), pl.BlockSpec((1, 1, heads * nt, rows), lambda i, c: (i, c, 0, 0))],
        out_specs=[st_spec(w), o_spec, o_spec],
        out_shape=[jax.ShapeDtypeStruct(state_t.shape, F32), jax.ShapeDtypeStruct((b, nt, 1024), F32),
                   jax.ShapeDtypeStruct((b, nt, 1024), F32)],
        compiler_params=_cparams("parallel", "parallel"),
        name=f"dil_sample_d{d}",
    )(state_t, newp, qbd)


CHUNKS_PER_PAGE = PAGE_SIZE // CMP_STRIDE


def _compress_kernel(tbl_ref, *refs, native, n_pages):
    del tbl_ref
    pages = refs[:n_pages + 1]
    perm_ref, w1_ref, pre_ref, w2_ref, out_ref, xp_ref, shift_ref = refs[n_pages + 1:]
    nch = n_pages * CHUNKS_PER_PAGE
    perm = perm_ref[...]
    for j, p in enumerate(pages):
        if native:
            for st in range(2):
                xt = p[0, st].reshape(NSA_KV_WIDTH, PAGE_SIZE).astype(BF16)
                xp_ref[j, :, st * 256:(st + 1) * 256] = lax.dot_general(perm, xt, _NT, preferred_element_type=F32)
        else:
            xp_ref[j] = jnp.dot(perm, p[0].astype(BF16), preferred_element_type=F32)
    pad = jnp.zeros((CHUNKS_PER_PAGE, 512), F32)
    acc = [[None, None], [None, None]]
    for s in range(CMP_STRIDE):
        rs = slice(s * CHUNKS_PER_PAGE, (s + 1) * CHUNKS_PER_PAGE)
        x = jnp.concatenate([xp_ref[j, rs, :] for j in range(n_pages + 1)] + [pad], axis=0).astype(BF16)
        for st in range(2):
            xs = x[:, st * 256:(st + 1) * 256]
            for r in range(2):
                part = jnp.dot(xs, w1_ref[st, r, s], preferred_element_type=F32)
                acc[st][r] = part if acc[st][r] is None else acc[st][r] + part
    outs = []
    for st in range(2):
        shift_ref[...] = acc[st][1]
        pre = acc[st][0][0:nch] + shift_ref[pl.ds(1, nch), :] + pre_ref[:, st * 256:(st + 1) * 256]
        outs.append(jnp.dot(_silu(pre).astype(BF16), w2_ref[st], preferred_element_type=F32))
    out_ref[0] = jnp.concatenate(outs, axis=1)


def _blockdiag4(a):
    eye = jnp.eye(NSA_KV_HEADS, dtype=a.dtype)
    out = a[..., None, :, None, :] * eye[:, None, :, None]
    return out.reshape(a.shape[:-2] + (NSA_KV_HEADS * a.shape[-2], NSA_KV_HEADS * a.shape[-1]))


def _cmp_bias_kernel(pos_ref, w1_ref, b1_ref, out_ref):
    for st in range(2):
        pos = jnp.broadcast_to(pos_ref[st], (8, CMP_BLOCK * HEAD_DIM)).astype(BF16)
        out_ref[st] = b1_ref[st] + jnp.dot(pos, w1_ref[st].astype(BF16), preferred_element_type=F32)


def _compress(src, table, cmp_w1, cmp_b1, cmp_w2, cmp_pos, native):
    b, npg = table.shape
    n_pages = 32 if npg % 32 == 0 else 16
    steps = npg // n_pages
    r = CMP_BLOCK // CMP_STRIDE
    w1r = cmp_w1.reshape(2, r, CMP_STRIDE, HEAD_DIM, HEAD_DIM)
    w1bd = _blockdiag4(w1r).astype(BF16)
    w2bd = _blockdiag4(cmp_w2).astype(BF16)
    pre = pl.pallas_call(
        _cmp_bias_kernel,
        out_shape=jax.ShapeDtypeStruct((2, 8, HEAD_DIM), F32), name="cmp_bias",
    )(cmp_pos.reshape(2, 1, CMP_BLOCK * HEAD_DIM), cmp_w1, cmp_b1.reshape(2, 1, HEAD_DIM))
    pre = jnp.tile(pre[:, 0, :], (1, NSA_KV_HEADS)).reshape(1, 512)
    flat = table.reshape(-1)
    blk = (1, 2, NSA_KV_HEADS, HEAD_DIM, PAGE_SIZE) if native else (1, PAGE_SIZE, 512)
    out_row = jnp.arange(PAGE_SIZE)
    src_row = (out_row % CHUNKS_PER_PAGE) * CMP_STRIDE + out_row // CHUNKS_PER_PAGE
    perm = (src_row[:, None] == jnp.arange(PAGE_SIZE)[None, :]).astype(BF16)

    def page_spec(j):
        return pl.BlockSpec(
            blk, lambda bi, si, tbl: (tbl[bi * npg + jnp.minimum(si * n_pages + j, npg - 1)],) + (0,) * (len(blk) - 1))

    const = lambda a: pl.BlockSpec(a.shape, lambda bi, si, tbl: (0,) * a.ndim)
    nch = n_pages * CHUNKS_PER_PAGE
    return pl.pallas_call(
        functools.partial(_compress_kernel, native=native, n_pages=n_pages),
        grid_spec=pltpu.PrefetchScalarGridSpec(
            num_scalar_prefetch=1, grid=(b, steps),
            in_specs=[page_spec(j) for j in range(n_pages + 1)] + [const(perm), const(w1bd), const(pre), const(w2bd)],
            out_specs=pl.BlockSpec((1, nch, 512), lambda bi, si, tbl: (bi, si, 0)),
            scratch_shapes=[pltpu.VMEM((n_pages + 1, PAGE_SIZE, 512), F32),
                            pltpu.VMEM((nch + 2 * CHUNKS_PER_PAGE, 256), F32)]),
        out_shape=jax.ShapeDtypeStruct((b, npg * CHUNKS_PER_PAGE, 512), F32),
        compiler_params=_cparams("parallel", "parallel"),
        name="nsa_compress",
    )(flat, *([src] * (n_pages + 1)), perm, w1bd, pre, w2bd)


def _select_bias(imp_ref, n_blk, n_rows, q_pos):
    cols = q_pos.shape[1]
    imp_s = imp_ref[pl.ds(7, n_rows, stride=4), :]
    for e in range(1, 5):
        imp_s = imp_s + imp_ref[pl.ds(7 + e, n_rows, stride=4), :]
    blk = lax.broadcasted_iota(jnp.int32, (n_rows, cols), 0)
    cur = q_pos // SEL_BLOCK
    valid = blk * SEL_BLOCK <= q_pos
    forced = jnp.logical_or(jnp.logical_or(blk == 0, blk == cur), blk == cur - 1)
    score = jnp.where(valid, imp_s, -1.0) + jnp.where(forced, FORCE_BONUS, 0.0)
    work = jnp.where(jnp.logical_and(blk < n_blk, jnp.logical_not(forced)), score, -jnp.inf)
    bias = jnp.where(forced, 0.0, NEG_INF)
    for _ in range(min(SEL_TOP_N, n_blk) - 3):
        mx = jnp.max(work, axis=0, keepdims=True)
        first = jnp.min(jnp.where(work == mx, blk, n_rows), axis=0, keepdims=True)
        pick = blk == first
        bias = jnp.where(pick, 0.0, bias)
        work = jnp.where(pick, -jnp.inf, work)
    return bias


def _nsa_prompt_kernel(qraw_ref, qrot_ref, kc_ref, vct_ref, ks_ref, vst_ref, kw_ref, vwt_ref,
                       gate_ref, o_ref, imp_ref, qaug_ref, sca_ref, scb_ref, *, n_cmp_pad, n_sel):
    qb = pl.program_id(1)
    s0 = qb * Q_TILE
    cols = NSA_GROUP * Q_TILE

    def stack(ref):
        x = ref[...].astype(F32)
        return jnp.concatenate([x[:, r * HEAD_DIM:(r + 1) * HEAD_DIM] for r in range(NSA_GROUP)],
                               axis=0).astype(BF16)

    q_raw, q_rot = stack(qraw_ref), stack(qrot_ref)
    t_col = lax.broadcasted_iota(jnp.int32, (1, cols), 1) % Q_TILE
    q_pos = s0 + t_col

    s_c = lax.dot_general(kc_ref[0], q_raw, _NT, preferred_element_type=F32)
    c_end = lax.broadcasted_iota(jnp.int32, (n_cmp_pad, 1), 0) * CMP_STRIDE + (CMP_BLOCK - 1)
    p_c = _softmax_cols(s_c, c_end <= q_pos)
    o_c = jnp.dot(vct_ref[0], p_c.astype(BF16), preferred_element_type=F32)

    imp = p_c[:, 0:Q_TILE]
    for r in range(1, NSA_GROUP):
        imp = imp + p_c[:, r * Q_TILE:(r + 1) * Q_TILE]
    imp_ref[0:8, :] = jnp.zeros((8, Q_TILE), F32)
    imp_ref[8:8 + n_cmp_pad, :] = imp
    imp_ref[8 + n_cmp_pad:16 + n_cmp_pad, :] = jnp.zeros((8, Q_TILE), F32)
    bias = _select_bias(imp_ref, n_sel, n_sel, q_pos[:, 0:Q_TILE])

    lane = lax.broadcasted_iota(jnp.int32, (Q_TILE, LANES), 1)
    n_tile = -(-n_sel // LANES)
    if n_tile * LANES > n_sel:
        bias = jnp.concatenate([bias, jnp.full((n_tile * LANES - n_sel, Q_TILE), NEG_INF, F32)], axis=0)
    x = qrot_ref[...].astype(F32) * LOG2_E
    q_lo = []
    for r in range(NSA_GROUP):
        pair = x[:, (r // 2) * LANES:(r // 2 + 1) * LANES]
        q_lo.append(pltpu.roll(pair, HEAD_DIM, 1) if r % 2 else pair)
    for sp in range(qaug_ref.shape[0]):
        b_t = bias[(sp // 2) * LANES:(sp // 2 + 1) * LANES].T
        b_hi = b_t if sp % 2 else pltpu.roll(b_t, HEAD_DIM, 1)
        for r in range(NSA_GROUP):
            qaug_ref[sp, r * Q_TILE:(r + 1) * Q_TILE, :] = jnp.where(lane < HEAD_DIM, q_lo[r], b_hi).astype(BF16)

    span_groups = HEAD_DIM * SEL_BLOCK // (SEL_HALVES * SEL_HALF)

    def scores(kt, sc_ref):
        q_aug = qaug_ref[kt // span_groups]
        k0 = pl.multiple_of(kt * (SEL_HALVES * SEL_HALF), SEL_HALVES * SEL_HALF)
        s_all = lax.dot_general(ks_ref[0, pl.ds(k0, SEL_HALVES * SEL_HALF), :], q_aug, _NT,
                                preferred_element_type=F32)
        top = None
        for i in range(SEL_HALVES):
            s = s_all[i * SEL_HALF:(i + 1) * SEL_HALF]
            sc_ref[i] = s
            top = s if top is None else jnp.maximum(top, s)
        return jnp.max(top, axis=0, keepdims=True)

    def update(kt, sc_ref, top, carry):
        m, l, acc = carry
        parts = [sc_ref[i] for i in range(SEL_HALVES)]
        if top is None:
            for i in range(SEL_HALVES):
                k_pos = (kt * SEL_HALVES + i) * SEL_HALF + lax.broadcasted_iota(jnp.int32, (SEL_HALF, 1), 0)
                parts[i] = jnp.where(k_pos <= q_pos, parts[i], NEG_INF)
                top = parts[i] if top is None else jnp.maximum(top, parts[i])
            top = jnp.max(top, axis=0, keepdims=True)
        m_new = jnp.maximum(m, top)
        alpha = jnp.exp2(m - m_new)
        acc = alpha * acc
        p_sum = None
        for i, s in enumerate(parts):
            p = jnp.exp2(s - m_new)
            p_sum = p if p_sum is None else p_sum + p
            acc = acc + jnp.dot(vst_ref[0, kt * SEL_HALVES + i], p.astype(BF16), preferred_element_type=F32)
        return m_new, alpha * l + jnp.sum(p_sum, axis=0, keepdims=True), acc

    n_main = s0 // (SEL_HALVES * SEL_HALF)

    def pair_step(j, carry):
        m, l, acc, top_a = carry
        top_b = scores(2 * j + 1, scb_ref)
        m, l, acc = update(2 * j, sca_ref, top_a, (m, l, acc))
        top_a = scores(2 * j + 2, sca_ref)
        m, l, acc = update(2 * j + 1, scb_ref, top_b, (m, l, acc))
        return m, l, acc, top_a

    def odd_tail(carry):
        m, l, acc, top_a = carry
        scores(n_main, scb_ref)
        m, l, acc = update(n_main - 1, sca_ref, top_a, (m, l, acc))
        return update(n_main, scb_ref, None, (m, l, acc))

    def even_tail(carry):
        m, l, acc, _ = carry
        return update(n_main, sca_ref, None, (m, l, acc))

    carry = (jnp.full((1, cols), NEG_INF, F32), jnp.zeros((1, cols), F32), jnp.zeros((HEAD_DIM, cols), F32),
             scores(0, sca_ref))
    carry = lax.fori_loop(0, n_main // 2, pair_step, carry)
    _, l, acc = lax.cond(n_main % 2 == 1, odd_tail, even_tail, carry)
    o_s = acc * (1.0 / l)

    n_w = NSA_WINDOW + Q_TILE
    w0 = jnp.maximum(s0 - NSA_WINDOW, 0)
    s_w = lax.dot_general(kw_ref[0, pl.ds(pl.multiple_of(w0, Q_TILE), n_w), :], q_rot, _NT,
                          preferred_element_type=F32)
    dpos = q_pos - (w0 + lax.broadcasted_iota(jnp.int32, (n_w, 1), 0))
    m_w = jnp.logical_and(dpos >= 0, dpos < NSA_WINDOW)
    p_w = _softmax_cols(s_w, m_w).astype(BF16)
    o_w = jnp.zeros((HEAD_DIM, cols), F32)
    for i in range(n_w // Q_TILE):
        o_w = o_w + jnp.dot(vwt_ref[0, w0 // Q_TILE + i], p_w[i * Q_TILE:(i + 1) * Q_TILE],
                            preferred_element_type=F32)

    def gate(b):
        gb = gate_ref[0, b]
        return jnp.concatenate([gb[r:r + 1, :] for r in range(NSA_GROUP)], axis=1)

    o_t = gate(0) * o_c + gate(1) * o_s + gate(2) * o_w
    for pr in range(NSA_GROUP // 2):
        two = jnp.concatenate([o_t[:, (2 * pr) * Q_TILE:(2 * pr + 1) * Q_TILE],
                               o_t[:, (2 * pr + 1) * Q_TILE:(2 * pr + 2) * Q_TILE]], axis=0)
        o_ref[:, pr * LANES:(pr + 1) * LANES] = two.T


def _nsa_prompt_attend(q_raw, q_rot, gate_t, kc_vc, ks_h, vs_t, kw_h, vw_t):
    t = q_raw.shape[0]
    assert t % (SEL_HALVES * SEL_HALF) == 0 and t >= NSA_WINDOW + Q_TILE
    nq = t // Q_TILE
    n_cmp_pad = kc_vc.shape[0]
    n_sel = t // SEL_BLOCK
    heads = lambda a: a.reshape(a.shape[0], NSA_KV_HEADS, HEAD_DIM).transpose(1, 0, 2)
    kc_h = heads(kc_vc[:, 0:256].astype(BF16))
    vc_t = heads(kc_vc[:, 256:512].astype(BF16)).transpose(0, 2, 1)
    gate4 = gate_t.reshape(NSA_KV_HEADS, 4, 8, t)
    per_head = lambda a: pl.BlockSpec((1,) + a.shape[1:], lambda g, i: (g,) + (0,) * (a.ndim - 1))
    qspec = pl.BlockSpec((Q_TILE, NSA_GROUP * HEAD_DIM), lambda g, i: (i, g))
    return pl.pallas_call(
        functools.partial(_nsa_prompt_kernel, n_cmp_pad=n_cmp_pad, n_sel=n_sel),
        grid=(NSA_KV_HEADS, nq),
        in_specs=[qspec, qspec, per_head(kc_h), per_head(vc_t), per_head(ks_h), per_head(vs_t),
                  per_head(kw_h), per_head(vw_t),
                  pl.BlockSpec((1, 4, 8, Q_TILE), lambda g, i: (g, 0, 0, i))],
        out_specs=qspec,
        out_shape=jax.ShapeDtypeStruct((t, 1024), F32),
        scratch_shapes=[pltpu.VMEM((n_cmp_pad + 16, Q_TILE), F32),
                        pltpu.VMEM((-(-n_sel // HEAD_DIM), NSA_GROUP * Q_TILE, LANES), BF16),
                        pltpu.VMEM((SEL_HALVES, SEL_HALF, NSA_GROUP * Q_TILE), F32),
                        pltpu.VMEM((SEL_HALVES, SEL_HALF, NSA_GROUP * Q_TILE), F32)],
        compiler_params=_cparams("parallel", "arbitrary"),
        name="nsa_prompt_attn",
    )(q_raw, q_rot, kc_h, vc_t, ks_h, vs_t, kw_h, vw_t, gate4)


SMP_PAGES = 16
BLOCKS_PER_STEP = SMP_PAGES * PAGE_SIZE // SEL_BLOCK


def _nsa_sample_kernel(tbl_ref, *refs, n_cmp_pad, n_sel, n_sel_pad, n_past, nt):
    del tbl_ref
    pages = refs[:SMP_PAGES]
    (qraw_ref, qrot_ref, kcvc_ref, knew_ref, wst_ref, wnew_ref, gate_ref, spread_ref, o_ref, wout_ref,
     imp_ref, bias2_ref, m_ref, l_ref, acc_ref, oc_ref) = refs[SMP_PAGES:]
    si = pl.program_id(1)
    q_rot = qrot_ref[0]
    lane = lax.broadcasted_iota(jnp.int32, (LANES, LANES), 1)
    t_row = lax.broadcasted_iota(jnp.int32, (LANES, 1), 0) % nt
    new0 = LANES - nt

    @pl.when(si == 0)
    def _():
        t_col = lax.broadcasted_iota(jnp.int32, (1, LANES), 1) % nt
        q_pos = n_past + t_col
        kc = kcvc_ref[0, :, 0:256].astype(BF16)
        s_c = lax.dot_general(kc, qraw_ref[0], _NT, preferred_element_type=F32)
        c_end = lax.broadcasted_iota(jnp.int32, (n_cmp_pad, 1), 0) * CMP_STRIDE + (CMP_BLOCK - 1)
        p_c = _softmax_cols(s_c, jnp.logical_and(c_end <= q_pos, c_end < n_past))
        o_c = jnp.zeros((LANES, NSA_KV_WIDTH), F32)
        for c in range(n_cmp_pad // LANES):
            rs = slice(c * LANES, (c + 1) * LANES)
            o_c = o_c + jnp.dot(p_c[rs].T.astype(BF16), kcvc_ref[0, rs, 256:512].astype(BF16),
                                preferred_element_type=F32)
        oc_ref[...] = o_c
        imp = p_c
        for r in range(1, NSA_GROUP):
            imp = imp + pltpu.roll(p_c, r * (LANES // NSA_GROUP), 1)
        imp_ref[0:8, :] = jnp.zeros((8, LANES), F32)
        imp_ref[8:8 + n_cmp_pad, :] = imp
        imp_ref[8 + n_cmp_pad:, :] = jnp.zeros((imp_ref.shape[0] - 8 - n_cmp_pad, LANES), F32)
        bias_t = _select_bias(imp_ref, n_sel, n_sel_pad, q_pos)
        n_grp = bias2_ref.shape[0]
        pad_rows = n_grp * BLOCKS_PER_STEP - n_sel_pad
        bias_t = jnp.concatenate([bias_t, jnp.full((pad_rows, LANES), NEG_INF, F32)], axis=0)
        fill = jnp.zeros((LANES - BLOCKS_PER_STEP, LANES), F32)
        for s in range(n_grp):
            grp = bias_t[s * BLOCKS_PER_STEP:(s + 1) * BLOCKS_PER_STEP]
            bias2_ref[s] = jnp.concatenate([grp, fill], axis=0).T
        m_ref[...] = jnp.full(m_ref.shape, NEG_INF, F32)
        l_ref[...] = jnp.zeros(l_ref.shape, F32)
        acc_ref[...] = jnp.zeros_like(acc_ref)

    def online(s_parts, v_parts):
        m_old = m_ref[:, 0:1]
        top = s_parts[0]
        for s in s_parts[1:]:
            top = jnp.maximum(top, s)
        m_new = jnp.maximum(m_old, jnp.max(top, axis=1, keepdims=True))
        alpha = jnp.exp(m_old - m_new)
        acc = alpha * acc_ref[...]
        p_sum = None
        for s, v in zip(s_parts, v_parts):
            p = jnp.exp(s - m_new)
            p_sum = p if p_sum is None else p_sum + p
            acc = acc + lax.dot_general(p.astype(BF16), v, _NT, preferred_element_type=F32)
        l_new = alpha * l_ref[:, 0:1] + jnp.sum(p_sum, axis=1, keepdims=True)
        m_ref[...] = jnp.broadcast_to(m_new, m_ref.shape)
        l_ref[...] = jnp.broadcast_to(l_new, l_ref.shape)
        acc_ref[...] = acc

    b2 = bias2_ref[si].astype(BF16)
    s_parts, v_parts = [], []
    for j, p in enumerate(pages):
        kt = p[0, 0].reshape(NSA_KV_WIDTH, PAGE_SIZE).astype(BF16)
        s = (jnp.dot(q_rot, kt, preferred_element_type=F32)
             + jnp.dot(b2, spread_ref[:, j * PAGE_SIZE:(j + 1) * PAGE_SIZE], preferred_element_type=F32))
        s_parts.append(s)
        v_parts.append(p[0, 1].reshape(NSA_KV_WIDTH, PAGE_SIZE).astype(BF16))
    online(s_parts, v_parts)

    @pl.when(si == pl.num_programs(1) - 1)
    def _():
        s_n = jnp.dot(q_rot, knew_ref[0, 0].astype(BF16), preferred_element_type=F32)
        last = bias2_ref[n_past // SEL_BLOCK // BLOCKS_PER_STEP]
        lb = (n_past // SEL_BLOCK) % BLOCKS_PER_STEP
        ok = jnp.logical_and(lane <= t_row, lane < nt)
        online([jnp.where(ok, s_n + last[:, lb:lb + 1], NEG_INF)], [knew_ref[0, 1].astype(BF16)])
        o_s = acc_ref[...] * (1.0 / l_ref[:, 0:1])

        n_buf = wst_ref.shape[-1]
        kwt, vwt = wst_ref[0, 0].reshape(NSA_KV_WIDTH, n_buf), wst_ref[0, 1].reshape(NSA_KV_WIDTH, n_buf)
        kwn, vwn = wnew_ref[0, 0], wnew_ref[0, 1]
        s_w = jnp.dot(q_rot, kwt.astype(BF16), preferred_element_type=F32)
        s_wn = jnp.dot(q_rot, kwn.astype(BF16), preferred_element_type=F32)
        dpos = n_buf + t_row - lax.broadcasted_iota(jnp.int32, (1, n_buf), 1)
        dpos_n = t_row - (lane - new0)
        p_w, p_wn, _ = _softmax_rows2(s_w, jnp.logical_and(dpos >= 0, dpos < NSA_WINDOW),
                                      s_wn, jnp.logical_and(lane >= new0, dpos_n >= 0))
        o_w = (lax.dot_general(p_w.astype(BF16), vwt.astype(BF16), _NT, preferred_element_type=F32)
               + lax.dot_general(p_wn.astype(BF16), vwn.astype(BF16), _NT, preferred_element_type=F32))
        g = gate_ref[0]
        o_ref[0] = g[:, 0:1] * oc_ref[...] + g[:, 1:2] * o_s + g[:, 2:3] * o_w

        is_new = lax.broadcasted_iota(jnp.int32, (NSA_KV_WIDTH, LANES), 1) >= new0
        for st, (old, new) in enumerate(((kwt, kwn), (vwt, vwn))):
            rolled = pltpu.roll(old, n_buf - nt, 1)
            tail = jnp.where(is_new, new, rolled[:, n_buf - LANES:])
            full = jnp.concatenate([rolled[:, :n_buf - LANES], tail], axis=1)
            wout_ref[0, st] = full.reshape(NSA_KV_HEADS, HEAD_DIM, n_buf)


def _nsa_sample_attend(q_raw, q_rot, gate_t, kc_vc, knew, cache_t, page_table, win_t, wnew):
    b, nt = q_raw.shape[0], q_raw.shape[1]
    npg = page_table.shape[1]
    n_past = npg * PAGE_SIZE
    n_cmp_pad = kc_vc.shape[1]
    n_sel = -(-(n_past + nt) // SEL_BLOCK)
    n_sel_pad = -(-n_sel // 8) * 8
    n_grp = -(-n_sel_pad // BLOCKS_PER_STEP)
    eye = jnp.eye(NSA_KV_HEADS, dtype=BF16)

    def blockdiag(q):
        q5 = q.reshape(b, nt, NSA_KV_HEADS, NSA_GROUP, HEAD_DIM).transpose(0, 3, 2, 1, 4)
        q6 = q5[:, :, :, :, None, :] * eye[None, None, :, None, :, None]
        return q6.reshape(b, LANES, NSA_KV_WIDTH)

    gt = gate_t.reshape(NSA_KV_HEADS, 4, 8, b, nt)[:, :3, :NSA_GROUP].transpose(3, 2, 0, 4, 1)
    gt = jnp.pad(gt.reshape(b, LANES, 3), ((0, 0), (0, 0), (0, LANES - 3)))
    flat = page_table.reshape(-1)

    def page_spec(j):
        return pl.BlockSpec((1, 2, NSA_KV_HEADS, HEAD_DIM, PAGE_SIZE),
                            lambda bi, si, tbl: (tbl[bi * npg + si * SMP_PAGES + j], 1, 0, 0, 0))

    per_b = lambda a: pl.BlockSpec((1,) + a.shape[1:], lambda bi, si, tbl: (bi,) + (0,) * (a.ndim - 1))
    qr, qo = blockdiag(q_raw), blockdiag(q_rot)
    spread = (jnp.arange(LANES)[:, None] == jnp.arange(SMP_PAGES * PAGE_SIZE)[None, :] // SEL_BLOCK).astype(BF16)
    o, wout = pl.pallas_call(
        functools.partial(_nsa_sample_kernel, n_cmp_pad=n_cmp_pad, n_sel=n_sel, n_sel_pad=n_sel_pad,
                          n_past=n_past, nt=nt),
        grid_spec=pltpu.PrefetchScalarGridSpec(
            num_scalar_prefetch=1, grid=(b, npg // SMP_PAGES),
            in_specs=[page_spec(j) for j in range(SMP_PAGES)] + [
                per_b(qr), per_b(qo), per_b(kc_vc), per_b(knew), per_b(win_t), per_b(wnew), per_b(gt),
                pl.BlockSpec(spread.shape, lambda bi, si, tbl: (0, 0))],
            out_specs=[pl.BlockSpec((1, LANES, NSA_KV_WIDTH), lambda bi, si, tbl: (bi, 0, 0)), per_b(win_t)],
            scratch_shapes=[pltpu.VMEM((4 * n_sel_pad + 16, LANES), F32),
                            pltpu.VMEM((n_grp, LANES, LANES), F32),
                            pltpu.VMEM((LANES, LANES), F32), pltpu.VMEM((LANES, LANES), F32),
                            pltpu.VMEM((LANES, NSA_KV_WIDTH), F32), pltpu.VMEM((LANES, NSA_KV_WIDTH), F32)]),
        out_shape=[jax.ShapeDtypeStruct((b, LANES, NSA_KV_WIDTH), F32), jax.ShapeDtypeStruct(win_t.shape, F32)],
        compiler_params=_cparams("parallel", "arbitrary"),
        name="nsa_sample_attn",
    )(flat, *([cache_t] * SMP_PAGES), qr, qo, kc_vc, knew, win_t, wnew, gt, spread)
    o6 = o.reshape(b, NSA_GROUP, NSA_KV_HEADS, nt, NSA_KV_HEADS, HEAD_DIM)
    o5 = jnp.stack([o6[:, :, gg, :, gg, :] for gg in range(NSA_KV_HEADS)], axis=2)
    return o5.transpose(0, 3, 2, 1, 4).reshape(b, nt, 1024), wout


def _rows_last(a):
    n = a.ndim
    return a.transpose(tuple(range(n - 4)) + (n - 3, n - 2, n - 1, n - 4))


def _rows_first(a):
    n = a.ndim
    return a.transpose(tuple(range(n - 4)) + (n - 1, n - 4, n - 3, n - 2))


def _nsa_prompt_layer(x, g, w_in, cmp, w_out):
    t = x.shape[1]
    x2 = x.reshape(t, D_MODEL)
    (q_raw, q_rot, rows_t, cmp_src, ks_h, vs_t, win_t, kw_h, vw_t, sz, gate_t) = _nsa_project(
        x2, g, w_in, jnp.arange(t, dtype=jnp.int32))
    npg = t // PAGE_SIZE
    table = jnp.arange(npg, dtype=jnp.int32).reshape(1, npg)
    kc_vc = _compress(cmp_src.reshape(npg, PAGE_SIZE, 512), table, *cmp, native=False)[0]
    o = _nsa_prompt_attend(q_raw, q_rot, gate_t, kc_vc, ks_h, vs_t, kw_h, vw_t)
    y = _nsa_out(o, sz, x2, w_out, 512)
    keep = min(NSA_WINDOW, t)
    rows = _rows_first(rows_t.reshape(1, 1, 4, NSA_KV_HEADS, HEAD_DIM, t))
    win = _rows_first(win_t[:, t - keep:].reshape(1, 1, 2, NSA_KV_HEADS, HEAD_DIM, keep))
    return y.reshape(1, t, D_MODEL), rows, win


def _nsa_sample_layer(x, cache_l, page_table, win_buf, g, w_in, cmp, w_out):
    b, nt, _ = x.shape
    m = b * nt
    n_past = page_table.shape[1] * PAGE_SIZE
    x2 = x.reshape(m, D_MODEL)
    pos = jnp.tile(n_past + jnp.arange(nt, dtype=jnp.int32), b)
    (q_raw, q_rot, rows_t, _, _, _, win_t, _, _, sz, gate_t) = _nsa_project(x2, g, w_in, pos)
    cache_t = _rows_last(cache_l)
    kc_vc = _compress(cache_t, page_table, *cmp, native=True)
    rows_b = rows_t.reshape(4, NSA_KV_WIDTH, b, nt).transpose(2, 0, 1, 3)
    knew = jnp.pad(rows_b[:, 2:4], ((0, 0), (0, 0), (0, 0), (0, LANES - nt)))
    wnew = jnp.pad(win_t.reshape(2, NSA_KV_WIDTH, b, nt).transpose(2, 0, 1, 3),
                   ((0, 0), (0, 0), (0, 0), (LANES - nt, 0)))
    o, wout = _nsa_sample_attend(q_raw.reshape(b, nt, 1024), q_rot.reshape(b, nt, 1024), gate_t, kc_vc,
                                 knew, cache_t, page_table, _rows_last(win_buf), wnew)
    y = _nsa_out(o.reshape(m, 1024), sz, x2, w_out, m)
    rows = rows_t.T.reshape(b, nt, 4, NSA_KV_HEADS, HEAD_DIM)
    return y.reshape(b, nt, D_MODEL), rows, _rows_first(wout)


def _dil_prompt_layer(x, g, w_in, w_out, gf):
    t = x.shape[1]
    x2 = x.reshape(t, D_MODEL)
    keeps = [min(w, t) for w, _ in DIL_PAIRS]
    proj = _dil_project(x2, g, w_in, jnp.arange(t, dtype=jnp.int32), keeps, [d for _, d in DIL_PAIRS],
                        with_kvb=True)
    outs, lses, new = [], [], []
    for (q, kvb, kv_t, *_), (w, d), keep in zip(proj, DIL_PAIRS, keeps):
        o, lse = _dil_prompt_attend(q, kvb, d)
        outs.append(o)
        lses.append(lse)
        new.append(_rows_first(kv_t.reshape(1, 1, 2, DIL_HEADS, HEAD_DIM, keep)))
    y = _dil_out(outs, lses, proj[-1][3], x2, w_out, gf, 256)
    return y.reshape(1, t, D_MODEL), new


def _dil_sample_layer(x, bufs, n_past, g, w_in, w_out, gf):
    b, nt, _ = x.shape
    m = b * nt
    x2 = x.reshape(m, D_MODEL)
    pos = jnp.tile(n_past + jnp.arange(nt, dtype=jnp.int32), b)
    proj = _dil_project(x2, g, w_in, pos, [m] * len(DIL_PAIRS), [1] * len(DIL_PAIRS), with_kvb=False)
    outs, lses, new = [], [], []
    for (q, kv_t, *_), buf, (w, d) in zip(proj, bufs, DIL_PAIRS):
        heads = _ds_heads(buf.shape[1])
        eye = jnp.eye(heads, dtype=BF16)
        nc = DIL_HEADS // heads
        newp = kv_t.reshape(2, DIL_HEADS, HEAD_DIM, b, nt).transpose(3, 0, 1, 2, 4)
        newp = jnp.pad(newp, ((0, 0),) * 4 + ((LANES - nt, 0),))
        q5 = q.reshape(b, nt, nc, heads, HEAD_DIM).transpose(0, 2, 3, 1, 4)
        qbd = q5[:, :, :, :, None, :] * eye[None, None, :, None, :, None]
        qbd = qbd.reshape(b, nc, heads * nt, heads * HEAD_DIM)
        ost, o, lse = _dil_sample_attend(_rows_last(buf), newp, qbd, d)
        outs.append(o.reshape(m, 1024))
        lses.append(lse.reshape(m, 1024))
        new.append(_rows_first(ost))
    y = _dil_out(outs, lses, proj[-1][2], x2, w_out, gf, m)
    return y.reshape(b, nt, D_MODEL), new


def kernel(x_prompt, x_sample, cache_nsa_kv, state_nsa_win, state_dil_g0, state_dil_g1, state_dil_g2,
           page_table, nsa_norm, nsa_w_in, nsa_cmp_w1, nsa_cmp_b1, nsa_cmp_w2, nsa_cmp_pos, nsa_w_out,
           dil_norm, dil_w_in, dil_w_out, final_norm):
    n_past = page_table.shape[1] * PAGE_SIZE
    cmp = (nsa_cmp_w1[0], nsa_cmp_b1[0], nsa_cmp_w2[0], nsa_cmp_pos[0])
    y_p, rows_p, win_p = _nsa_prompt_layer(x_prompt, nsa_norm[0], nsa_w_in[0], cmp, nsa_w_out[0])
    y_s, rows_s, win_s = _nsa_sample_layer(x_sample, cache_nsa_kv[0], page_table, state_nsa_win[0],
                                           nsa_norm[0], nsa_w_in[0], cmp, nsa_w_out[0])
    y_p, new_p = _dil_prompt_layer(y_p, dil_norm[0], dil_w_in[0], dil_w_out[0], final_norm)
    y_s, new_s = _dil_sample_layer(y_s, (state_dil_g0[0], state_dil_g1[0], state_dil_g2[0]), n_past,
                                   dil_norm[0], dil_w_in[0], dil_w_out[0], final_norm)
    add = lambda a: a[None]
    return (y_p, y_s, rows_p, add(rows_s), win_p, add(win_s),
            new_p[0], add(new_s[0]), new_p[1], add(new_s[1]), new_p[2], add(new_s[2]))
```

```python
import functools

import jax
import jax.numpy as jnp
from jax import lax
from jax.experimental import pallas as pl
from jax.experimental.pallas import tpu as pltpu

F32 = jnp.float32
BF16 = jnp.bfloat16

D_MODEL = 1024
HEAD_DIM = 64
ROT_DIM = HEAD_DIM // 4
ROPE_THETA = 500000.0
NORM_EPS = 1e-6
NEG_INF = -1e30
Q_SCALE = HEAD_DIM ** -0.5
LOG2_E = 1.4426950408889634
PAGE_SIZE = 128

NSA_KV_HEADS = 4
NSA_GROUP = 4
NSA_KV_WIDTH = NSA_KV_HEADS * HEAD_DIM
CMP_BLOCK = 32
CMP_STRIDE = 16
SEL_BLOCK = 64
SEL_TOP_N = 16
NSA_WINDOW = 512
FORCE_BONUS = 1e4
DIL_PAIRS = ((128, 1), (512, 4), (2048, 16))
DIL_HEADS = 16

LANES = 128
Q_TILE = 128
SEL_HALF = 256
SEL_HALVES = 4
PROJ_TILE = 256
VMEM_LIMIT = 56 * 1024 * 1024

_NT = (((1,), (1,)), ((), ()))


def _cparams(*sem):
    return pltpu.CompilerParams(dimension_semantics=sem, vmem_limit_bytes=VMEM_LIMIT)


def _rope_tables(pos):
    half = ROT_DIM // 2
    inv = ROPE_THETA ** (-jnp.arange(half, dtype=F32) / half)
    posf = pos.astype(F32)
    in_head = jnp.arange(LANES) % HEAD_DIM
    ang = posf[:, None] * inv[in_head % half][None, :]
    cos, sin = jnp.cos(ang), jnp.sin(ang)
    c = jnp.where(in_head < ROT_DIM, cos, 1.0)
    s1 = jnp.where(in_head < half, -sin, 0.0)
    s2 = jnp.where(jnp.logical_and(in_head >= half, in_head < ROT_DIM), sin, 0.0)
    ang_t = inv[:, None] * posf[None, :]
    return jnp.concatenate([c, s1, s2], axis=1), jnp.cos(ang_t), jnp.sin(ang_t)


def _rope(y, tab_ref):
    c, s1, s2 = tab_ref[:, 0:128], tab_ref[:, 128:256], tab_ref[:, 256:384]
    out = []
    for j in range(y.shape[1] // LANES):
        yc = y[:, j * LANES:(j + 1) * LANES]
        out.append(yc * c + pltpu.roll(yc, LANES - 8, 1) * s1 + pltpu.roll(yc, 8, 1) * s2)
    return out[0] if len(out) == 1 else jnp.concatenate(out, axis=1)


def _rope_t(y, cos, sin):
    out = []
    for h in range(y.shape[0] // HEAD_DIM):
        b = h * HEAD_DIM
        x1, x2 = y[b:b + 8], y[b + 8:b + 16]
        out += [x1 * cos - x2 * sin, x2 * cos + x1 * sin, y[b + 16:b + HEAD_DIM]]
    return jnp.concatenate(out, axis=0)


def _normed_bf16(x_ref, g_ref):
    x = x_ref[...]
    ms = jnp.mean(x * x, axis=-1, keepdims=True)
    return ((x * lax.rsqrt(ms + NORM_EPS)) * g_ref[...]).astype(BF16)


def _silu(z):
    return z * jax.nn.sigmoid(z)


def _softmax_cols(s, mask):
    s = jnp.where(mask, s, NEG_INF)
    m = jnp.max(s, axis=0, keepdims=True)
    e = jnp.exp(s - m)
    den = jnp.sum(e, axis=0, keepdims=True)
    return e * jnp.where(m > 0.5 * NEG_INF, 1.0 / den, 0.0)


def _softmax_rows2(s_a, m_a, s_b, m_b):
    s_a = jnp.where(m_a, s_a, NEG_INF)
    s_b = jnp.where(m_b, s_b, NEG_INF)
    m = jnp.maximum(jnp.max(s_a, axis=1, keepdims=True), jnp.max(s_b, axis=1, keepdims=True))
    e_a = jnp.where(m_a, jnp.exp(s_a - m), 0.0)
    e_b = jnp.where(m_b, jnp.exp(s_b - m), 0.0)
    den = jnp.sum(e_a, axis=1, keepdims=True) + jnp.sum(e_b, axis=1, keepdims=True)
    inv = 1.0 / jnp.maximum(den, 1e-30)
    return e_a * inv, e_b * inv, m + jnp.log(den)


def _nsa_proj_kernel(x_ref, g_ref, wnn_ref, wt_ref, tab_ref, cos_ref, sin_ref,
                     qraw_ref, qrot_ref, rowst_ref, cmp_ref, ks_ref, vst_ref, wint_ref, kw_ref, vwt_ref,
                     sz_ref, gatet_ref):
    tm = x_ref.shape[0]
    hb = _normed_bf16(x_ref, g_ref)
    nn = lambda a, b: jnp.dot(hb, wnn_ref[:, a:b], preferred_element_type=F32)
    nt = lambda a, b: lax.dot_general(wt_ref[a:b, :], hb, _NT, preferred_element_type=F32)
    cos, sin = cos_ref[...], sin_ref[...]
    q = nn(0, 1024) * Q_SCALE
    qraw_ref[...] = q.astype(BF16)
    qrot_ref[...] = _rope(q, tab_ref).astype(BF16)
    kcs = nn(1024, 1792)
    cmp_ref[...] = kcs[:, :512]
    ks = _rope(kcs[:, 512:768], tab_ref)
    kw = _rope(nn(1792, 2048), tab_ref)
    row = pl.program_id(0) * tm + lax.broadcasted_iota(jnp.int32, (tm, LANES), 0)
    lane = lax.broadcasted_iota(jnp.int32, (tm, LANES), 1)
    mark = jnp.where(lane - HEAD_DIM == (row // SEL_BLOCK) % HEAD_DIM, 1.0, 0.0)
    for g in range(NSA_KV_HEADS):
        pair = ks[:, (g // 2) * LANES:(g // 2 + 1) * LANES]
        if g % 2:
            pair = pltpu.roll(pair, HEAD_DIM, 1)
        ks_ref[g] = jnp.where(lane < HEAD_DIM, pair, mark).astype(BF16)
        kw_ref[g] = kw[:, g * HEAD_DIM:(g + 1) * HEAD_DIM].astype(BF16)
    sz_ref[...] = _silu(nn(2048, 3072))
    rt = nt(0, 1024)
    rowst_ref[...] = jnp.concatenate([rt[:512], _rope_t(rt[512:768], cos, sin), rt[768:]], axis=0)
    vs = rt[768:1024].astype(BF16)
    wt = nt(1024, 1536)
    wint_ref[...] = jnp.concatenate([_rope_t(wt[:256], cos, sin), wt[256:]], axis=0)
    vw = wt[256:512].astype(BF16)
    for g in range(NSA_KV_HEADS):
        hs = slice(g * HEAD_DIM, (g + 1) * HEAD_DIM)
        for c in range(tm // SEL_HALF):
            vst_ref[g, c] = vs[hs, c * SEL_HALF:(c + 1) * SEL_HALF]
        for c in range(tm // Q_TILE):
            vwt_ref[g, c] = vw[hs, c * Q_TILE:(c + 1) * Q_TILE]
    gatet_ref[...] = jax.nn.sigmoid(nt(1536, 1664))


def _nsa_project(x2, g, w_in, pos):
    m = x2.shape[0]
    tm = PROJ_TILE
    w_nn = jnp.concatenate([w_in[:, 0:1024], w_in[:, 1024:1792], w_in[:, 2048:2304], w_in[:, 2608:3632]],
                           axis=1).astype(BF16)
    gl = w_in[:, 2560:2608].T.reshape(NSA_KV_HEADS, NSA_GROUP, 3, D_MODEL).transpose(0, 2, 1, 3)
    gl = jnp.pad(gl, ((0, 0), (0, 1), (0, 8 - NSA_GROUP), (0, 0))).reshape(LANES, D_MODEL)
    w_t = jnp.concatenate([w_in[:, 1024:2048].T, w_in[:, 2048:2560].T, gl], axis=0).astype(BF16)
    tab, cos_t, sin_t = _rope_tables(pos)
    g2 = g.reshape(1, D_MODEL)
    row = lambda n: pl.BlockSpec((tm, n), lambda i: (i, 0))
    col = lambda n: pl.BlockSpec((n, tm), lambda i: (0, i))
    full = lambda a: pl.BlockSpec(a.shape, lambda i: (0,) * a.ndim)
    out = [
        (row(1024), (m, 1024), BF16), (row(1024), (m, 1024), BF16),
        (col(1024), (1024, m), F32), (row(512), (m, 512), F32),
        (pl.BlockSpec((NSA_KV_HEADS, tm, LANES), lambda i: (0, i, 0)), (NSA_KV_HEADS, m, LANES), BF16),
        (pl.BlockSpec((NSA_KV_HEADS, tm // SEL_HALF, HEAD_DIM, SEL_HALF), lambda i: (0, i, 0, 0)),
         (NSA_KV_HEADS, m // SEL_HALF, HEAD_DIM, SEL_HALF), BF16),
        (col(512), (512, m), F32),
        (pl.BlockSpec((NSA_KV_HEADS, tm, HEAD_DIM), lambda i: (0, i, 0)), (NSA_KV_HEADS, m, HEAD_DIM), BF16),
        (pl.BlockSpec((NSA_KV_HEADS, tm // Q_TILE, HEAD_DIM, Q_TILE), lambda i: (0, i, 0, 0)),
         (NSA_KV_HEADS, m // Q_TILE, HEAD_DIM, Q_TILE), BF16),
        (row(1024), (m, 1024), F32), (col(LANES), (LANES, m), F32),
    ]
    return pl.pallas_call(
        _nsa_proj_kernel,
        grid=(m // tm,),
        in_specs=[row(D_MODEL), full(g2), full(w_nn), full(w_t), row(384), col(8), col(8)],
        out_specs=[o[0] for o in out],
        out_shape=[jax.ShapeDtypeStruct(o[1], o[2]) for o in out],
        compiler_params=_cparams("parallel"),
        name="nsa_proj",
    )(x2, g2, w_nn, w_t, tab, cos_t, sin_t)


def _dil_proj_kernel(x_ref, g_ref, wnn_ref, wt_ref, tab_ref, cos_ref, sin_ref, perm_ref, *out_refs,
                     with_kvb, with_z, tail_steps, tail_lanes):
    out_refs = list(out_refs)
    q_ref = out_refs.pop(0)
    kvb_ref = out_refs.pop(0) if with_kvb else None
    kvt_ref = out_refs.pop(0)
    sz_ref = out_refs.pop(0) if with_z else None
    tm = x_ref.shape[0]
    hb = _normed_bf16(x_ref, g_ref)
    nn = lambda a, b: jnp.dot(hb, wnn_ref[:, a:b], preferred_element_type=F32)

    def put(ref, cols, y):
        y = y.astype(BF16)
        if len(ref.shape) == 3:
            y = jnp.dot(perm_ref[...], y, preferred_element_type=F32).astype(BF16)
            ref[:, :, cols] = y.reshape(ref.shape[0], ref.shape[1], y.shape[1])
        else:
            ref[:, cols] = y

    put(q_ref, slice(0, 1024), _rope(nn(0, 1024) * Q_SCALE, tab_ref))
    if with_kvb:
        put(kvb_ref, slice(0, 1024), _rope(nn(1024, 2048), tab_ref))
        put(kvb_ref, slice(1024, 2048), nn(2048, 3072))
    if with_z:
        sz_ref[...] = _silu(nn(3072, 4096))

    @pl.when(pl.program_id(0) >= pl.num_programs(0) - tail_steps)
    def _():
        kt = lax.dot_general(wt_ref[0:1024, :], hb, _NT, preferred_element_type=F32)
        kt = _rope_t(kt, cos_ref[...], sin_ref[...])
        vt = lax.dot_general(wt_ref[1024:2048, :], hb, _NT, preferred_element_type=F32)
        kvt_ref[0:1024, :] = kt[:, tm - tail_lanes:]
        kvt_ref[1024:2048, :] = vt[:, tm - tail_lanes:]


def _dil_project(x2, g, w_in, pos, keeps, dils, with_kvb):
    m = x2.shape[0]
    tm = PROJ_TILE
    n = m // tm
    tab, cos_t, sin_t = _rope_tables(pos)
    g2 = g.reshape(1, D_MODEL)
    row = lambda c: pl.BlockSpec((tm, c), lambda i: (i, 0))
    col = lambda c: pl.BlockSpec((c, tm), lambda i: (0, i))
    full = lambda a: pl.BlockSpec(a.shape, lambda i: (0,) * a.ndim)
    outs = []
    for gi, (keep, dil) in enumerate(zip(keeps, dils)):
        last = gi == len(keeps) - 1

        def rows_out(c):
            if dil == 1:
                return row(c), (m, c)
            return pl.BlockSpec((dil, tm // dil, c), lambda i: (0, i, 0)), (dil, m // dil, c)

        o_row = jnp.arange(tm)
        perm = ((o_row % (tm // dil)) * dil + o_row // (tm // dil))[:, None] == jnp.arange(tm)[None, :]
        perm = perm.astype(BF16)
        wg = w_in[:, 3072 * gi:3072 * (gi + 1)]
        cols = [wg[:, :1024]] + ([wg[:, 1024:]] if with_kvb else []) + ([w_in[:, 9216:]] if last else [])
        if not with_kvb and last:
            cols.insert(1, jnp.zeros((D_MODEL, 2048), F32))
        w_nn = jnp.concatenate(cols, axis=1).astype(BF16)
        w_t = wg[:, 1024:].T.astype(BF16)
        tail_steps = max(keep // tm, 1)
        tail_lanes = min(keep, tm)
        spec = [rows_out(1024) + (BF16,)]
        if with_kvb:
            spec.append(rows_out(2048) + (BF16,))
        spec.append((pl.BlockSpec((2048, tail_lanes), lambda i: (0, jnp.maximum(i - (n - tail_steps), 0))),
                     (2048, keep), F32))
        if last:
            spec.append((row(1024), (m, 1024), F32))
        outs.append(pl.pallas_call(
            functools.partial(_dil_proj_kernel, with_kvb=with_kvb, with_z=last, tail_steps=tail_steps,
                              tail_lanes=tail_lanes),
            grid=(n,),
            in_specs=[row(D_MODEL), full(g2), full(w_nn), full(w_t), row(384), col(8), col(8), full(perm)],
            out_specs=[s[0] for s in spec],
            out_shape=[jax.ShapeDtypeStruct(s[1], s[2]) for s in spec],
            compiler_params=_cparams("arbitrary"),
            name=f"dil_proj_g{gi}",
        )(x2, g2, w_nn, w_t, tab, cos_t, sin_t, perm))
    return outs


def _nsa_out_kernel(o_ref, sz_ref, x_ref, w_ref, y_ref):
    a = (o_ref[...] * sz_ref[...]).astype(BF16)
    y_ref[...] = x_ref[...] + jnp.dot(a, w_ref[...], preferred_element_type=F32)


def _dil_out_kernel(o0_ref, o1_ref, o2_ref, l0_ref, l1_ref, l2_ref, sz_ref, x_ref, w_ref, gf_ref, y_ref):
    l0, l1, l2 = l0_ref[...], l1_ref[...], l2_ref[...]
    m = jnp.maximum(jnp.maximum(l0, l1), l2)
    e0, e1, e2 = jnp.exp(l0 - m), jnp.exp(l1 - m), jnp.exp(l2 - m)
    inv = 1.0 / (e0 + e1 + e2)
    o = (e0 * inv) * o0_ref[...] + (e1 * inv) * o1_ref[...] + (e2 * inv) * o2_ref[...]
    a = (o * sz_ref[...]).astype(BF16)
    y = x_ref[...] + jnp.dot(a, w_ref[...], preferred_element_type=F32)
    ms = jnp.mean(y * y, axis=-1, keepdims=True)
    y_ref[...] = (y * lax.rsqrt(ms + NORM_EPS)) * gf_ref[...]


def _nsa_out(o, sz, x2, w_out, tm):
    m = x2.shape[0]
    row = pl.BlockSpec((tm, D_MODEL), lambda i: (i, 0))
    w = w_out.astype(BF16)
    return pl.pallas_call(
        _nsa_out_kernel, grid=(m // tm,),
        in_specs=[row, row, row, pl.BlockSpec(w.shape, lambda i: (0, 0))],
        out_specs=row, out_shape=jax.ShapeDtypeStruct((m, D_MODEL), F32),
        compiler_params=_cparams("parallel"), name="nsa_out",
    )(o, sz, x2, w)


def _dil_out(outs, lses, sz, x2, w_out, gf, tm):
    m = x2.shape[0]
    row = pl.BlockSpec((tm, D_MODEL), lambda i: (i, 0))
    w = w_out.astype(BF16)
    gf2 = gf.reshape(1, D_MODEL)
    return pl.pallas_call(
        _dil_out_kernel, grid=(m // tm,),
        in_specs=[row] * 8 + [pl.BlockSpec(w.shape, lambda i: (0, 0)), pl.BlockSpec(gf2.shape, lambda i: (0, 0))],
        out_specs=row, out_shape=jax.ShapeDtypeStruct((m, D_MODEL), F32),
        compiler_params=_cparams("parallel"), name="dil_out",
    )(*outs, *lses, sz, x2, w, gf2)


def _dil_prompt_kernel(q_ref, kp_ref, kc_ref, vp_ref, vc_ref, o_ref, lse_ref):
    ib = pl.program_id(1)
    tq = lax.broadcasted_iota(jnp.int32, (Q_TILE, 2 * Q_TILE), 0)
    kk = lax.broadcasted_iota(jnp.int32, (Q_TILE, 2 * Q_TILE), 1)
    diff = kk - tq
    lo_key = jnp.where(ib > 0, 0, Q_TILE)
    mask = jnp.logical_and(jnp.logical_and(diff >= 0, diff <= Q_TILE), kk >= lo_key)
    lo = lax.broadcasted_iota(jnp.int32, (Q_TILE, LANES), 1) < HEAD_DIM
    for pr in range(DIL_HEADS // 2):
        cs = slice(pr * LANES, (pr + 1) * LANES)
        q2 = q_ref[0, :, cs]
        k2 = jnp.concatenate([kp_ref[0, :, cs], kc_ref[0, :, cs]], axis=0)
        v2 = jnp.concatenate([vp_ref[0, :, cs], vc_ref[0, :, cs]], axis=0)
        outs, lses = [], []
        for half in range(2):
            keep = lo if half == 0 else jnp.logical_not(lo)
            qm = jnp.where(keep, q2, jnp.zeros_like(q2))
            s = lax.dot_general(qm, k2, _NT, preferred_element_type=F32)
            s = jnp.where(mask, s, NEG_INF)
            m = jnp.max(s, axis=-1, keepdims=True)
            e = jnp.exp(s - m)
            den = jnp.sum(e, axis=-1, keepdims=True)
            outs.append(jnp.dot(e.astype(BF16), v2, preferred_element_type=F32) * (1.0 / den))
            lses.append(jnp.broadcast_to(m + jnp.log(den), (Q_TILE, LANES)))
        o_ref[:, cs] = jnp.where(lo, outs[0], outs[1])
        lse_ref[:, cs] = jnp.where(lo, lses[0], lses[1])


def _dil_prompt_attend(q, kvb, d):
    n = q.shape[-2]
    t = n * d
    qv = q.reshape(d, n, 1024)
    kvv = kvb.reshape(d, n, 2048)
    blk = lambda f: pl.BlockSpec((Q_TILE, 1024), f)
    blk3 = lambda f: pl.BlockSpec((1, Q_TILE, 1024), f)
    prev = lambda i: jnp.maximum(i - 1, 0)
    out = pl.pallas_call(
        _dil_prompt_kernel,
        grid=(d, n // Q_TILE),
        in_specs=[blk3(lambda r, i: (r, i, 0)),
                  blk3(lambda r, i: (r, prev(i), 0)), blk3(lambda r, i: (r, i, 0)),
                  blk3(lambda r, i: (r, prev(i), 1)), blk3(lambda r, i: (r, i, 1))],
        out_specs=[blk(lambda r, i: (i, r)), blk(lambda r, i: (i, r))],
        out_shape=[jax.ShapeDtypeStruct((n, d * 1024), F32)] * 2,
        compiler_params=_cparams("parallel", "arbitrary"),
        name=f"dil_prompt_d{d}",
    )(qv, kvv, kvv, kvv, kvv)
    return out[0].reshape(t, 1024), out[1].reshape(t, 1024)


def _ds_heads(w):
    return min(DIL_HEADS, max(4, 8192 // w))


def _dil_sample_kernel(st_ref, newp_ref, qbd_ref, ost_ref, o_ref, lse_ref, *, w, d):
    nt = o_ref.shape[1]
    heads = st_ref.shape[2]
    rows = heads * HEAD_DIM
    nq = heads * nt
    new0 = LANES - nt
    kt, vt = st_ref[0, 0].reshape(rows, w), st_ref[0, 1].reshape(rows, w)
    kn, vn = newp_ref[0, 0].reshape(rows, LANES), newp_ref[0, 1].reshape(rows, LANES)
    qbd = qbd_ref[0, 0]
    s_st = jnp.dot(qbd, kt.astype(BF16), preferred_element_type=F32)
    s_nw = jnp.dot(qbd, kn.astype(BF16), preferred_element_type=F32)
    t_row = lax.broadcasted_iota(jnp.int32, (nq, 1), 0) % nt
    diff_st = (w + t_row) - lax.broadcasted_iota(jnp.int32, (1, w), 1)
    m_st = jnp.logical_and((diff_st & (d - 1)) == 0, diff_st <= w)
    i_nw = lax.broadcasted_iota(jnp.int32, (1, LANES), 1) - new0
    diff_nw = t_row - i_nw
    m_nw = jnp.logical_and(jnp.logical_and(i_nw >= 0, diff_nw >= 0), (diff_nw & (d - 1)) == 0)
    p_st, p_nw, lse = _softmax_rows2(s_st, m_st, s_nw, m_nw)
    acc = (lax.dot_general(p_st.astype(BF16), vt.astype(BF16), _NT, preferred_element_type=F32)
           + lax.dot_general(p_nw.astype(BF16), vn.astype(BF16), _NT, preferred_element_type=F32))
    diag = (lax.broadcasted_iota(jnp.int32, (nq, rows), 0) // nt
            == lax.broadcasted_iota(jnp.int32, (nq, rows), 1) // HEAD_DIM)

    def fold(a):
        a = jnp.where(diag, a, 0.0)
        out = a[0:nt]
        for h in range(1, heads):
            out = out + a[h * nt:(h + 1) * nt]
        return out

    o_ref[0] = fold(acc)
    lse_ref[0] = fold(jnp.broadcast_to(lse, (nq, rows)))
    is_new = lax.broadcasted_iota(jnp.int32, (rows, LANES), 1) >= new0
    for si, (old, new) in enumerate(((kt, kn), (vt, vn))):
        rolled = pltpu.roll(old, w - nt, 1)
        tail = jnp.where(is_new, new, rolled[:, w - LANES:])
        full = tail if w == LANES else jnp.concatenate([rolled[:, :w - LANES], tail], axis=1)
        ost_ref[0, si] = full.reshape(heads, HEAD_DIM, w)


def _dil_sample_attend(state_t, newp, qbd, d):
    b, w = state_t.shape[0], state_t.shape[-1]
    heads = _ds_heads(w)
    nt = qbd.shape[2] // heads
    rows = heads * HEAD_DIM
    st_spec = lambda n: pl.BlockSpec((1, 2, heads, HEAD_DIM, n), lambda i, c: (i, 0, c, 0, 0))
    o_spec = pl.BlockSpec((1, nt, rows), lambda i, c: (i, 0, c))
    return pl.pallas_call(
        functools.partial(_dil_sample_kernel, w=w, d=d),
        grid=(b, DIL_HEADS // heads),
        in_specs=[st_spec(w), st_spec(LANES), pl.BlockSpec((1, 1, heads * nt, rows), lambda i, c: (i, c, 0, 0))],
        out_specs=[st_spec(w), o_spec, o_spec],
        out_shape=[jax.ShapeDtypeStruct(state_t.shape, F32), jax.ShapeDtypeStruct((b, nt, 1024), F32),
                   jax.ShapeDtypeStruct((b, nt, 1024), F32)],
        compiler_params=_cparams("parallel", "parallel"),
        name=f"dil_sample_d{d}",
    )(state_t, newp, qbd)


CHUNKS_PER_PAGE = PAGE_SIZE // CMP_STRIDE


def _compress_kernel(tbl_ref, *refs, native, n_pages):
    del tbl_ref
    pages = refs[:n_pages + 1]
    perm_ref, w1_ref, pre_ref, w2_ref, out_ref, xp_ref, shift_ref = refs[n_pages + 1:]
    nch = n_pages * CHUNKS_PER_PAGE
    perm = perm_ref[...]
    for j, p in enumerate(pages):
        if native:
            for st in range(2):
                xt = p[0, st].reshape(NSA_KV_WIDTH, PAGE_SIZE).astype(BF16)
                xp_ref[j, :, st * 256:(st + 1) * 256] = lax.dot_general(perm, xt, _NT, preferred_element_type=F32)
        else:
            xp_ref[j] = jnp.dot(perm, p[0].astype(BF16), preferred_element_type=F32)
    pad = jnp.zeros((CHUNKS_PER_PAGE, 512), F32)
    acc = [[None, None], [None, None]]
    for s in range(CMP_STRIDE):
        rs = slice(s * CHUNKS_PER_PAGE, (s + 1) * CHUNKS_PER_PAGE)
        x = jnp.concatenate([xp_ref[j, rs, :] for j in range(n_pages + 1)] + [pad], axis=0).astype(BF16)
        for st in range(2):
            xs = x[:, st * 256:(st + 1) * 256]
            for r in range(2):
                part = jnp.dot(xs, w1_ref[st, r, s], preferred_element_type=F32)
                acc[st][r] = part if acc[st][r] is None else acc[st][r] + part
    outs = []
    for st in range(2):
        shift_ref[...] = acc[st][1]
        pre = acc[st][0][0:nch] + shift_ref[pl.ds(1, nch), :] + pre_ref[:, st * 256:(st + 1) * 256]
        outs.append(jnp.dot(_silu(pre).astype(BF16), w2_ref[st], preferred_element_type=F32))
    out_ref[0] = jnp.concatenate(outs, axis=1)


def _blockdiag4(a):
    eye = jnp.eye(NSA_KV_HEADS, dtype=a.dtype)
    out = a[..., None, :, None, :] * eye[:, None, :, None]
    return out.reshape(a.shape[:-2] + (NSA_KV_HEADS * a.shape[-2], NSA_KV_HEADS * a.shape[-1]))


def _cmp_bias_kernel(pos_ref, w1_ref, b1_ref, out_ref):
    for st in range(2):
        pos = jnp.broadcast_to(pos_ref[st], (8, CMP_BLOCK * HEAD_DIM)).astype(BF16)
        out_ref[st] = b1_ref[st] + jnp.dot(pos, w1_ref[st].astype(BF16), preferred_element_type=F32)


def _compress(src, table, cmp_w1, cmp_b1, cmp_w2, cmp_pos, native):
    b, npg = table.shape
    n_pages = 32 if npg % 32 == 0 else 16
    steps = npg // n_pages
    r = CMP_BLOCK // CMP_STRIDE
    w1r = cmp_w1.reshape(2, r, CMP_STRIDE, HEAD_DIM, HEAD_DIM)
    w1bd = _blockdiag4(w1r).astype(BF16)
    w2bd = _blockdiag4(cmp_w2).astype(BF16)
    pre = pl.pallas_call(
        _cmp_bias_kernel,
        out_shape=jax.ShapeDtypeStruct((2, 8, HEAD_DIM), F32), name="cmp_bias",
    )(cmp_pos.reshape(2, 1, CMP_BLOCK * HEAD_DIM), cmp_w1, cmp_b1.reshape(2, 1, HEAD_DIM))
    pre = jnp.tile(pre[:, 0, :], (1, NSA_KV_HEADS)).reshape(1, 512)
    flat = table.reshape(-1)
    blk = (1, 2, NSA_KV_HEADS, HEAD_DIM, PAGE_SIZE) if native else (1, PAGE_SIZE, 512)
    out_row = jnp.arange(PAGE_SIZE)
    src_row = (out_row % CHUNKS_PER_PAGE) * CMP_STRIDE + out_row // CHUNKS_PER_PAGE
    perm = (src_row[:, None] == jnp.arange(PAGE_SIZE)[None, :]).astype(BF16)

    def page_spec(j):
        return pl.BlockSpec(
            blk, lambda bi, si, tbl: (tbl[bi * npg + jnp.minimum(si * n_pages + j, npg - 1)],) + (0,) * (len(blk) - 1))

    const = lambda a: pl.BlockSpec(a.shape, lambda bi, si, tbl: (0,) * a.ndim)
    nch = n_pages * CHUNKS_PER_PAGE
    return pl.pallas_call(
        functools.partial(_compress_kernel, native=native, n_pages=n_pages),
        grid_spec=pltpu.PrefetchScalarGridSpec(
            num_scalar_prefetch=1, grid=(b, steps),
            in_specs=[page_spec(j) for j in range(n_pages + 1)] + [const(perm), const(w1bd), const(pre), const(w2bd)],
            out_specs=pl.BlockSpec((1, nch, 512), lambda bi, si, tbl: (bi, si, 0)),
            scratch_shapes=[pltpu.VMEM((n_pages + 1, PAGE_SIZE, 512), F32),
                            pltpu.VMEM((nch + 2 * CHUNKS_PER_PAGE, 256), F32)]),
        out_shape=jax.ShapeDtypeStruct((b, npg * CHUNKS_PER_PAGE, 512), F32),
        compiler_params=_cparams("parallel", "parallel"),
        name="nsa_compress",
    )(flat, *([src] * (n_pages + 1)), perm, w1bd, pre, w2bd)


def _select_bias(imp_ref, n_blk, n_rows, q_pos):
    cols = q_pos.shape[1]
    imp_s = imp_ref[pl.ds(7, n_rows, stride=4), :]
    for e in range(1, 5):
        imp_s = imp_s + imp_ref[pl.ds(7 + e, n_rows, stride=4), :]
    blk = lax.broadcasted_iota(jnp.int32, (n_rows, cols), 0)
    cur = q_pos // SEL_BLOCK
    valid = blk * SEL_BLOCK <= q_pos
    forced = jnp.logical_or(jnp.logical_or(blk == 0, blk == cur), blk == cur - 1)
    score = jnp.where(valid, imp_s, -1.0) + jnp.where(forced, FORCE_BONUS, 0.0)
    work = jnp.where(jnp.logical_and(blk < n_blk, jnp.logical_not(forced)), score, -jnp.inf)
    bias = jnp.where(forced, 0.0, NEG_INF)
    for _ in range(min(SEL_TOP_N, n_blk) - 3):
        mx = jnp.max(work, axis=0, keepdims=True)
        first = jnp.min(jnp.where(work == mx, blk, n_rows), axis=0, keepdims=True)
        pick = blk == first
        bias = jnp.where(pick, 0.0, bias)
        work = jnp.where(pick, -jnp.inf, work)
    return bias


def _nsa_prompt_kernel(qraw_ref, qrot_ref, kc_ref, vct_ref, ks_ref, vst_ref, kw_ref, vwt_ref,
                       gate_ref, o_ref, imp_ref, qaug_ref, sca_ref, scb_ref, *, n_cmp_pad, n_sel):
    qb = pl.program_id(1)
    s0 = qb * Q_TILE
    cols = NSA_GROUP * Q_TILE

    def stack(ref):
        x = ref[...].astype(F32)
        return jnp.concatenate([x[:, r * HEAD_DIM:(r + 1) * HEAD_DIM] for r in range(NSA_GROUP)],
                               axis=0).astype(BF16)

    q_raw, q_rot = stack(qraw_ref), stack(qrot_ref)
    t_col = lax.broadcasted_iota(jnp.int32, (1, cols), 1) % Q_TILE
    q_pos = s0 + t_col

    s_c = lax.dot_general(kc_ref[0], q_raw, _NT, preferred_element_type=F32)
    c_end = lax.broadcasted_iota(jnp.int32, (n_cmp_pad, 1), 0) * CMP_STRIDE + (CMP_BLOCK - 1)
    p_c = _softmax_cols(s_c, c_end <= q_pos)
    o_c = jnp.dot(vct_ref[0], p_c.astype(BF16), preferred_element_type=F32)

    imp = p_c[:, 0:Q_TILE]
    for r in range(1, NSA_GROUP):
        imp = imp + p_c[:, r * Q_TILE:(r + 1) * Q_TILE]
    imp_ref[0:8, :] = jnp.zeros((8, Q_TILE), F32)
    imp_ref[8:8 + n_cmp_pad, :] = imp
    imp_ref[8 + n_cmp_pad:16 + n_cmp_pad, :] = jnp.zeros((8, Q_TILE), F32)
    bias = _select_bias(imp_ref, n_sel, n_sel, q_pos[:, 0:Q_TILE])

    lane = lax.broadcasted_iota(jnp.int32, (Q_TILE, LANES), 1)
    n_tile = -(-n_sel // LANES)
    if n_tile * LANES > n_sel:
        bias = jnp.concatenate([bias, jnp.full((n_tile * LANES - n_sel, Q_TILE), NEG_INF, F32)], axis=0)
    x = qrot_ref[...].astype(F32) * LOG2_E
    q_lo = []
    for r in range(NSA_GROUP):
        pair = x[:, (r // 2) * LANES:(r // 2 + 1) * LANES]
        q_lo.append(pltpu.roll(pair, HEAD_DIM, 1) if r % 2 else pair)
    for sp in range(qaug_ref.shape[0]):
        b_t = bias[(sp // 2) * LANES:(sp // 2 + 1) * LANES].T
        b_hi = b_t if sp % 2 else pltpu.roll(b_t, HEAD_DIM, 1)
        for r in range(NSA_GROUP):
            qaug_ref[sp, r * Q_TILE:(r + 1) * Q_TILE, :] = jnp.where(lane < HEAD_DIM, q_lo[r], b_hi).astype(BF16)

    span_groups = HEAD_DIM * SEL_BLOCK // (SEL_HALVES * SEL_HALF)

    def scores(kt, sc_ref):
        q_aug = qaug_ref[kt // span_groups]
        k0 = pl.multiple_of(kt * (SEL_HALVES * SEL_HALF), SEL_HALVES * SEL_HALF)
        s_all = lax.dot_general(ks_ref[0, pl.ds(k0, SEL_HALVES * SEL_HALF), :], q_aug, _NT,
                                preferred_element_type=F32)
        top = None
        for i in range(SEL_HALVES):
            s = s_all[i * SEL_HALF:(i + 1) * SEL_HALF]
            sc_ref[i] = s
            top = s if top is None else jnp.maximum(top, s)
        return jnp.max(top, axis=0, keepdims=True)

    def update(kt, sc_ref, top, carry):
        m, l, acc = carry
        parts = [sc_ref[i] for i in range(SEL_HALVES)]
        if top is None:
            for i in range(SEL_HALVES):
                k_pos = (kt * SEL_HALVES + i) * SEL_HALF + lax.broadcasted_iota(jnp.int32, (SEL_HALF, 1), 0)
                parts[i] = jnp.where(k_pos <= q_pos, parts[i], NEG_INF)
                top = parts[i] if top is None else jnp.maximum(top, parts[i])
            top = jnp.max(top, axis=0, keepdims=True)
        m_new = jnp.maximum(m, top)
        alpha = jnp.exp2(m - m_new)
        acc = alpha * acc
        p_sum = None
        for i, s in enumerate(parts):
            p = jnp.exp2(s - m_new)
            p_sum = p if p_sum is None else p_sum + p
            acc = acc + jnp.dot(vst_ref[0, kt * SEL_HALVES + i], p.astype(BF16), preferred_element_type=F32)
        return m_new, alpha * l + jnp.sum(p_sum, axis=0, keepdims=True), acc

    n_main = s0 // (SEL_HALVES * SEL_HALF)

    def pair_step(j, carry):
        m, l, acc, top_a = carry
        top_b = scores(2 * j + 1, scb_ref)
        m, l, acc = update(2 * j, sca_ref, top_a, (m, l, acc))
        top_a = scores(2 * j + 2, sca_ref)
        m, l, acc = update(2 * j + 1, scb_ref, top_b, (m, l, acc))
        return m, l, acc, top_a

    def odd_tail(carry):
        m, l, acc, top_a = carry
        scores(n_main, scb_ref)
        m, l, acc = update(n_main - 1, sca_ref, top_a, (m, l, acc))
        return update(n_main, scb_ref, None, (m, l, acc))

    def even_tail(carry):
        m, l, acc, _ = carry
        return update(n_main, sca_ref, None, (m, l, acc))

    carry = (jnp.full((1, cols), NEG_INF, F32), jnp.zeros((1, cols), F32), jnp.zeros((HEAD_DIM, cols), F32),
             scores(0, sca_ref))
    carry = lax.fori_loop(0, n_main // 2, pair_step, carry)
    _, l, acc = lax.cond(n_main % 2 == 1, odd_tail, even_tail, carry)
    o_s = acc * (1.0 / l)

    n_w = NSA_WINDOW + Q_TILE
    w0 = jnp.maximum(s0 - NSA_WINDOW, 0)
    s_w = lax.dot_general(kw_ref[0, pl.ds(pl.multiple_of(w0, Q_TILE), n_w), :], q_rot, _NT,
                          preferred_element_type=F32)
    dpos = q_pos - (w0 + lax.broadcasted_iota(jnp.int32, (n_w, 1), 0))
    m_w = lax.bitcast_convert_type(dpos, jnp.uint32) < NSA_WINDOW
    p_w = _softmax_cols(s_w, m_w).astype(BF16)
    o_w = jnp.zeros((HEAD_DIM, cols), F32)
    for i in range(n_w // Q_TILE):
        o_w = o_w + jnp.dot(vwt_ref[0, w0 // Q_TILE + i], p_w[i * Q_TILE:(i + 1) * Q_TILE],
                            preferred_element_type=F32)

    def gate(b):
        gb = gate_ref[0, b]
        return jnp.concatenate([gb[r:r + 1, :] for r in range(NSA_GROUP)], axis=1)

    o_t = gate(0) * o_c + gate(1) * o_s + gate(2) * o_w
    for pr in range(NSA_GROUP // 2):
        two = jnp.concatenate([o_t[:, (2 * pr) * Q_TILE:(2 * pr + 1) * Q_TILE],
                               o_t[:, (2 * pr + 1) * Q_TILE:(2 * pr + 2) * Q_TILE]], axis=0)
        o_ref[:, pr * LANES:(pr + 1) * LANES] = two.T


def _nsa_prompt_attend(q_raw, q_rot, gate_t, kc_vc, ks_h, vs_t, kw_h, vw_t):
    t = q_raw.shape[0]
    assert t % (SEL_HALVES * SEL_HALF) == 0 and t >= NSA_WINDOW + Q_TILE
    nq = t // Q_TILE
    n_cmp_pad = kc_vc.shape[0]
    n_sel = t // SEL_BLOCK
    heads = lambda a: a.reshape(a.shape[0], NSA_KV_HEADS, HEAD_DIM).transpose(1, 0, 2)
    kc_h = heads(kc_vc[:, 0:256].astype(BF16))
    vc_t = heads(kc_vc[:, 256:512].astype(BF16)).transpose(0, 2, 1)
    gate4 = gate_t.reshape(NSA_KV_HEADS, 4, 8, t)
    per_head = lambda a: pl.BlockSpec((1,) + a.shape[1:], lambda g, i: (g,) + (0,) * (a.ndim - 1))
    qspec = pl.BlockSpec((Q_TILE, NSA_GROUP * HEAD_DIM), lambda g, i: (i, g))
    return pl.pallas_call(
        functools.partial(_nsa_prompt_kernel, n_cmp_pad=n_cmp_pad, n_sel=n_sel),
        grid=(NSA_KV_HEADS, nq),
        in_specs=[qspec, qspec, per_head(kc_h), per_head(vc_t), per_head(ks_h), per_head(vs_t),
                  per_head(kw_h), per_head(vw_t),
                  pl.BlockSpec((1, 4, 8, Q_TILE), lambda g, i: (g, 0, 0, i))],
        out_specs=qspec,
        out_shape=jax.ShapeDtypeStruct((t, 1024), F32),
        scratch_shapes=[pltpu.VMEM((n_cmp_pad + 16, Q_TILE), F32),
                        pltpu.VMEM((-(-n_sel // HEAD_DIM), NSA_GROUP * Q_TILE, LANES), BF16),
                        pltpu.VMEM((SEL_HALVES, SEL_HALF, NSA_GROUP * Q_TILE), F32),
                        pltpu.VMEM((SEL_HALVES, SEL_HALF, NSA_GROUP * Q_TILE), F32)],
        compiler_params=_cparams("parallel", "arbitrary"),
        name="nsa_prompt_attn",
    )(q_raw, q_rot, kc_h, vc_t, ks_h, vs_t, kw_h, vw_t, gate4)


SMP_PAGES = 16
BLOCKS_PER_STEP = SMP_PAGES * PAGE_SIZE // SEL_BLOCK


def _nsa_sample_kernel(tbl_ref, *refs, n_cmp_pad, n_sel, n_sel_pad, n_past, nt):
    del tbl_ref
    pages = refs[:SMP_PAGES]
    (qraw_ref, qrot_ref, kcvc_ref, knew_ref, wst_ref, wnew_ref, gate_ref, spread_ref, o_ref, wout_ref,
     imp_ref, bias2_ref, m_ref, l_ref, acc_ref, oc_ref) = refs[SMP_PAGES:]
    si = pl.program_id(1)
    q_rot = qrot_ref[0]
    lane = lax.broadcasted_iota(jnp.int32, (LANES, LANES), 1)
    t_row = lax.broadcasted_iota(jnp.int32, (LANES, 1), 0) % nt
    new0 = LANES - nt

    @pl.when(si == 0)
    def _():
        t_col = lax.broadcasted_iota(jnp.int32, (1, LANES), 1) % nt
        q_pos = n_past + t_col
        kc = kcvc_ref[0, :, 0:256].astype(BF16)
        s_c = lax.dot_general(kc, qraw_ref[0], _NT, preferred_element_type=F32)
        c_end = lax.broadcasted_iota(jnp.int32, (n_cmp_pad, 1), 0) * CMP_STRIDE + (CMP_BLOCK - 1)
        p_c = _softmax_cols(s_c, jnp.logical_and(c_end <= q_pos, c_end < n_past))
        o_c = jnp.zeros((LANES, NSA_KV_WIDTH), F32)
        for c in range(n_cmp_pad // LANES):
            rs = slice(c * LANES, (c + 1) * LANES)
            o_c = o_c + jnp.dot(p_c[rs].T.astype(BF16), kcvc_ref[0, rs, 256:512].astype(BF16),
                                preferred_element_type=F32)
        oc_ref[...] = o_c
        imp = p_c
        for r in range(1, NSA_GROUP):
            imp = imp + pltpu.roll(p_c, r * (LANES // NSA_GROUP), 1)
        imp_ref[0:8, :] = jnp.zeros((8, LANES), F32)
        imp_ref[8:8 + n_cmp_pad, :] = imp
        imp_ref[8 + n_cmp_pad:, :] = jnp.zeros((imp_ref.shape[0] - 8 - n_cmp_pad, LANES), F32)
        bias_t = _select_bias(imp_ref, n_sel, n_sel_pad, q_pos)
        n_grp = bias2_ref.shape[0]
        pad_rows = n_grp * BLOCKS_PER_STEP - n_sel_pad
        bias_t = jnp.concatenate([bias_t, jnp.full((pad_rows, LANES), NEG_INF, F32)], axis=0)
        fill = jnp.zeros((LANES - BLOCKS_PER_STEP, LANES), F32)
        for s in range(n_grp):
            grp = bias_t[s * BLOCKS_PER_STEP:(s + 1) * BLOCKS_PER_STEP]
            bias2_ref[s] = jnp.concatenate([grp, fill], axis=0).T
        m_ref[...] = jnp.full(m_ref.shape, NEG_INF, F32)
        l_ref[...] = jnp.zeros(l_ref.shape, F32)
        acc_ref[...] = jnp.zeros_like(acc_ref)

    def online(s_parts, v_parts):
        m_old = m_ref[:, 0:1]
        top = s_parts[0]
        for s in s_parts[1:]:
            top = jnp.maximum(top, s)
        m_new = jnp.maximum(m_old, jnp.max(top, axis=1, keepdims=True))
        alpha = jnp.exp(m_old - m_new)
        acc = alpha * acc_ref[...]
        p_sum = None
        for s, v in zip(s_parts, v_parts):
            p = jnp.exp(s - m_new)
            p_sum = p if p_sum is None else p_sum + p
            acc = acc + lax.dot_general(p.astype(BF16), v, _NT, preferred_element_type=F32)
        l_new = alpha * l_ref[:, 0:1] + jnp.sum(p_sum, axis=1, keepdims=True)
        m_ref[...] = jnp.broadcast_to(m_new, m_ref.shape)
        l_ref[...] = jnp.broadcast_to(l_new, l_ref.shape)
        acc_ref[...] = acc

    b2 = bias2_ref[si].astype(BF16)
    s_parts, v_parts = [], []
    for j, p in enumerate(pages):
        kt = p[0, 0].reshape(NSA_KV_WIDTH, PAGE_SIZE).astype(BF16)
        s = (jnp.dot(q_rot, kt, preferred_element_type=F32)
             + jnp.dot(b2, spread_ref[:, j * PAGE_SIZE:(j + 1) * PAGE_SIZE], preferred_element_type=F32))
        s_parts.append(s)
        v_parts.append(p[0, 1].reshape(NSA_KV_WIDTH, PAGE_SIZE).astype(BF16))
    online(s_parts, v_parts)

    @pl.when(si == pl.num_programs(1) - 1)
    def _():
        s_n = jnp.dot(q_rot, knew_ref[0, 0].astype(BF16), preferred_element_type=F32)
        last = bias2_ref[n_past // SEL_BLOCK // BLOCKS_PER_STEP]
        lb = (n_past // SEL_BLOCK) % BLOCKS_PER_STEP
        ok = jnp.logical_and(lane <= t_row, lane < nt)
        online([jnp.where(ok, s_n + last[:, lb:lb + 1], NEG_INF)], [knew_ref[0, 1].astype(BF16)])
        o_s = acc_ref[...] * (1.0 / l_ref[:, 0:1])

        n_buf = wst_ref.shape[-1]
        kwt, vwt = wst_ref[0, 0].reshape(NSA_KV_WIDTH, n_buf), wst_ref[0, 1].reshape(NSA_KV_WIDTH, n_buf)
        kwn, vwn = wnew_ref[0, 0], wnew_ref[0, 1]
        s_w = jnp.dot(q_rot, kwt.astype(BF16), preferred_element_type=F32)
        s_wn = jnp.dot(q_rot, kwn.astype(BF16), preferred_element_type=F32)
        dpos = n_buf + t_row - lax.broadcasted_iota(jnp.int32, (1, n_buf), 1)
        dpos_n = t_row - (lane - new0)
        p_w, p_wn, _ = _softmax_rows2(s_w, jnp.logical_and(dpos >= 0, dpos < NSA_WINDOW),
                                      s_wn, jnp.logical_and(lane >= new0, dpos_n >= 0))
        o_w = (lax.dot_general(p_w.astype(BF16), vwt.astype(BF16), _NT, preferred_element_type=F32)
               + lax.dot_general(p_wn.astype(BF16), vwn.astype(BF16), _NT, preferred_element_type=F32))
        g = gate_ref[0]
        o_ref[0] = g[:, 0:1] * oc_ref[...] + g[:, 1:2] * o_s + g[:, 2:3] * o_w

        is_new = lax.broadcasted_iota(jnp.int32, (NSA_KV_WIDTH, LANES), 1) >= new0
        for st, (old, new) in enumerate(((kwt, kwn), (vwt, vwn))):
            rolled = pltpu.roll(old, n_buf - nt, 1)
            tail = jnp.where(is_new, new, rolled[:, n_buf - LANES:])
            full = jnp.concatenate([rolled[:, :n_buf - LANES], tail], axis=1)
            wout_ref[0, st] = full.reshape(NSA_KV_HEADS, HEAD_DIM, n_buf)


def _nsa_sample_attend(q_raw, q_rot, gate_t, kc_vc, knew, cache_t, page_table, win_t, wnew):
    b, nt = q_raw.shape[0], q_raw.shape[1]
    npg = page_table.shape[1]
    n_past = npg * PAGE_SIZE
    n_cmp_pad = kc_vc.shape[1]
    n_sel = -(-(n_past + nt) // SEL_BLOCK)
    n_sel_pad = -(-n_sel // 8) * 8
    n_grp = -(-n_sel_pad // BLOCKS_PER_STEP)
    eye = jnp.eye(NSA_KV_HEADS, dtype=BF16)

    def blockdiag(q):
        q5 = q.reshape(b, nt, NSA_KV_HEADS, NSA_GROUP, HEAD_DIM).transpose(0, 3, 2, 1, 4)
        q6 = q5[:, :, :, :, None, :] * eye[None, None, :, None, :, None]
        return q6.reshape(b, LANES, NSA_KV_WIDTH)

    gt = gate_t.reshape(NSA_KV_HEADS, 4, 8, b, nt)[:, :3, :NSA_GROUP].transpose(3, 2, 0, 4, 1)
    gt = jnp.pad(gt.reshape(b, LANES, 3), ((0, 0), (0, 0), (0, LANES - 3)))
    flat = page_table.reshape(-1)

    def page_spec(j):
        return pl.BlockSpec((1, 2, NSA_KV_HEADS, HEAD_DIM, PAGE_SIZE),
                            lambda bi, si, tbl: (tbl[bi * npg + si * SMP_PAGES + j], 1, 0, 0, 0))

    per_b = lambda a: pl.BlockSpec((1,) + a.shape[1:], lambda bi, si, tbl: (bi,) + (0,) * (a.ndim - 1))
    qr, qo = blockdiag(q_raw), blockdiag(q_rot)
    spread = (jnp.arange(LANES)[:, None] == jnp.arange(SMP_PAGES * PAGE_SIZE)[None, :] // SEL_BLOCK).astype(BF16)
    o, wout = pl.pallas_call(
        functools.partial(_nsa_sample_kernel, n_cmp_pad=n_cmp_pad, n_sel=n_sel, n_sel_pad=n_sel_pad,
                          n_past=n_past, nt=nt),
        grid_spec=pltpu.PrefetchScalarGridSpec(
            num_scalar_prefetch=1, grid=(b, npg // SMP_PAGES),
            in_specs=[page_spec(j) for j in range(SMP_PAGES)] + [
                per_b(qr), per_b(qo), per_b(kc_vc), per_b(knew), per_b(win_t), per_b(wnew), per_b(gt),
                pl.BlockSpec(spread.shape, lambda bi, si, tbl: (0, 0))],
            out_specs=[pl.BlockSpec((1, LANES, NSA_KV_WIDTH), lambda bi, si, tbl: (bi, 0, 0)), per_b(win_t)],
            scratch_shapes=[pltpu.VMEM((4 * n_sel_pad + 16, LANES), F32),
                            pltpu.VMEM((n_grp, LANES, LANES), F32),
                            pltpu.VMEM((LANES, LANES), F32), pltpu.VMEM((LANES, LANES), F32),
                            pltpu.VMEM((LANES, NSA_KV_WIDTH), F32), pltpu.VMEM((LANES, NSA_KV_WIDTH), F32)]),
        out_shape=[jax.ShapeDtypeStruct((b, LANES, NSA_KV_WIDTH), F32), jax.ShapeDtypeStruct(win_t.shape, F32)],
        compiler_params=_cparams("parallel", "arbitrary"),
        name="nsa_sample_attn",
    )(flat, *([cache_t] * SMP_PAGES), qr, qo, kc_vc, knew, win_t, wnew, gt, spread)
    o6 = o.reshape(b, NSA_GROUP, NSA_KV_HEADS, nt, NSA_KV_HEADS, HEAD_DIM)
    o5 = jnp.stack([o6[:, :, gg, :, gg, :] for gg in range(NSA_KV_HEADS)], axis=2)
    return o5.transpose(0, 3, 2, 1, 4).reshape(b, nt, 1024), wout


def _rows_last(a):
    n = a.ndim
    return a.transpose(tuple(range(n - 4)) + (n - 3, n - 2, n - 1, n - 4))


def _rows_first(a):
    n = a.ndim
    return a.transpose(tuple(range(n - 4)) + (n - 1, n - 4, n - 3, n - 2))


def _nsa_prompt_layer(x, g, w_in, cmp, w_out):
    t = x.shape[1]
    x2 = x.reshape(t, D_MODEL)
    (q_raw, q_rot, rows_t, cmp_src, ks_h, vs_t, win_t, kw_h, vw_t, sz, gate_t) = _nsa_project(
        x2, g, w_in, jnp.arange(t, dtype=jnp.int32))
    npg = t // PAGE_SIZE
    table = jnp.arange(npg, dtype=jnp.int32).reshape(1, npg)
    kc_vc = _compress(cmp_src.reshape(npg, PAGE_SIZE, 512), table, *cmp, native=False)[0]
    o = _nsa_prompt_attend(q_raw, q_rot, gate_t, kc_vc, ks_h, vs_t, kw_h, vw_t)
    y = _nsa_out(o, sz, x2, w_out, 512)
    keep = min(NSA_WINDOW, t)
    rows = _rows_first(rows_t.reshape(1, 1, 4, NSA_KV_HEADS, HEAD_DIM, t))
    win = _rows_first(win_t[:, t - keep:].reshape(1, 1, 2, NSA_KV_HEADS, HEAD_DIM, keep))
    return y.reshape(1, t, D_MODEL), rows, win


def _nsa_sample_layer(x, cache_l, page_table, win_buf, g, w_in, cmp, w_out):
    b, nt, _ = x.shape
    m = b * nt
    n_past = page_table.shape[1] * PAGE_SIZE
    x2 = x.reshape(m, D_MODEL)
    pos = jnp.tile(n_past + jnp.arange(nt, dtype=jnp.int32), b)
    (q_raw, q_rot, rows_t, _, _, _, win_t, _, _, sz, gate_t) = _nsa_project(x2, g, w_in, pos)
    cache_t = _rows_last(cache_l)
    kc_vc = _compress(cache_t, page_table, *cmp, native=True)
    rows_b = rows_t.reshape(4, NSA_KV_WIDTH, b, nt).transpose(2, 0, 1, 3)
    knew = jnp.pad(rows_b[:, 2:4], ((0, 0), (0, 0), (0, 0), (0, LANES - nt)))
    wnew = jnp.pad(win_t.reshape(2, NSA_KV_WIDTH, b, nt).transpose(2, 0, 1, 3),
                   ((0, 0), (0, 0), (0, 0), (LANES - nt, 0)))
    o, wout = _nsa_sample_attend(q_raw.reshape(b, nt, 1024), q_rot.reshape(b, nt, 1024), gate_t, kc_vc,
                                 knew, cache_t, page_table, _rows_last(win_buf), wnew)
    y = _nsa_out(o.reshape(m, 1024), sz, x2, w_out, m)
    rows = rows_t.T.reshape(b, nt, 4, NSA_KV_HEADS, HEAD_DIM)
    return y.reshape(b, nt, D_MODEL), rows, _rows_first(wout)


def _dil_prompt_layer(x, g, w_in, w_out, gf):
    t = x.shape[1]
    x2 = x.reshape(t, D_MODEL)
    keeps = [min(w, t) for w, _ in DIL_PAIRS]
    proj = _dil_project(x2, g, w_in, jnp.arange(t, dtype=jnp.int32), keeps, [d for _, d in DIL_PAIRS],
                        with_kvb=True)
    outs, lses, new = [], [], []
    for (q, kvb, kv_t, *_), (w, d), keep in zip(proj, DIL_PAIRS, keeps):
        o, lse = _dil_prompt_attend(q, kvb, d)
        outs.append(o)
        lses.append(lse)
        new.append(_rows_first(kv_t.reshape(1, 1, 2, DIL_HEADS, HEAD_DIM, keep)))
    y = _dil_out(outs, lses, proj[-1][3], x2, w_out, gf, 256)
    return y.reshape(1, t, D_MODEL), new


def _dil_sample_layer(x, bufs, n_past, g, w_in, w_out, gf):
    b, nt, _ = x.shape
    m = b * nt
    x2 = x.reshape(m, D_MODEL)
    pos = jnp.tile(n_past + jnp.arange(nt, dtype=jnp.int32), b)
    proj = _dil_project(x2, g, w_in, pos, [m] * len(DIL_PAIRS), [1] * len(DIL_PAIRS), with_kvb=False)
    outs, lses, new = [], [], []
    for (q, kv_t, *_), buf, (w, d) in zip(proj, bufs, DIL_PAIRS):
        heads = _ds_heads(buf.shape[1])
        eye = jnp.eye(heads, dtype=BF16)
        nc = DIL_HEADS // heads
        newp = kv_t.reshape(2, DIL_HEADS, HEAD_DIM, b, nt).transpose(3, 0, 1, 2, 4)
        newp = jnp.pad(newp, ((0, 0),) * 4 + ((LANES - nt, 0),))
        q5 = q.reshape(b, nt, nc, heads, HEAD_DIM).transpose(0, 2, 3, 1, 4)
        qbd = q5[:, :, :, :, None, :] * eye[None, None, :, None, :, None]
        qbd = qbd.reshape(b, nc, heads * nt, heads * HEAD_DIM)
        ost, o, lse = _dil_sample_attend(_rows_last(buf), newp, qbd, d)
        outs.append(o.reshape(m, 1024))
        lses.append(lse.reshape(m, 1024))
        new.append(_rows_first(ost))
    y = _dil_out(outs, lses, proj[-1][2], x2, w_out, gf, m)
    return y.reshape(b, nt, D_MODEL), new


def kernel(x_prompt, x_sample, cache_nsa_kv, state_nsa_win, state_dil_g0, state_dil_g1, state_dil_g2,
           page_table, nsa_norm, nsa_w_in, nsa_cmp_w1, nsa_cmp_b1, nsa_cmp_w2, nsa_cmp_pos, nsa_w_out,
           dil_norm, dil_w_in, dil_w_out, final_norm):
    n_past = page_table.shape[1] * PAGE_SIZE
    cmp = (nsa_cmp_w1[0], nsa_cmp_b1[0], nsa_cmp_w2[0], nsa_cmp_pos[0])
    y_p, rows_p, win_p = _nsa_prompt_layer(x_prompt, nsa_norm[0], nsa_w_in[0], cmp, nsa_w_out[0])
    y_s, rows_s, win_s = _nsa_sample_layer(x_sample, cache_nsa_kv[0], page_table, state_nsa_win[0],
                                           nsa_norm[0], nsa_w_in[0], cmp, nsa_w_out[0])
    y_p, new_p = _dil_prompt_layer(y_p, dil_norm[0], dil_w_in[0], dil_w_out[0], final_norm)
    y_s, new_s = _dil_sample_layer(y_s, (state_dil_g0[0], state_dil_g1[0], state_dil_g2[0]), n_past,
                                   dil_norm[0], dil_w_in[0], dil_w_out[0], final_norm)
    add = lambda a: a[None]
    return (y_p, y_s, rows_p, add(rows_s), win_p, add(win_s),
            new_p[0], add(new_s[0]), new_p[1], add(new_s[1]), new_p[2], add(new_s[2]))
```
